```python
import math, functools
import jax, jax.numpy as jnp
from jax import lax
import numpy as np

D_MODEL = 1024
BATCH = 16
SEQ = 2048
DEPTH = 2
DEC_BATCH = 32
DEC_SEQ = 8
PAST_LEN = 16384
PAGE_SIZE = 128

N_META = 16
C_CONV = D_MODEL
CONV_K = 31
D_INNER = 2 * D_MODEL
SSM_HEAD_DIM = 64
H_SSM = D_INNER // SSM_HEAD_DIM
N_GROUPS = 8
D_STATE = 128
SSM_CONV_K = 4
CONV_DIM = D_INNER + 2 * N_GROUPS * D_STATE
SSD_CHUNK = 128
HEAD_DIM = 64
ATTN_HEADS = D_MODEL // (2 * HEAD_DIM)
ATTN_W = ATTN_HEADS * 2 * HEAD_DIM
Q_BLOCK = 128
D_FF = 4 * D_MODEL
N_BRANCH = 3
EPS = 1e-6

IN_SEGMENTS = (('conv_glu', 2 * C_CONV), ('ssm_z', D_INNER), ('ssm_xbc', CONV_DIM), ('ssm_dt', H_SSM), ('attn_q', ATTN_W), ('attn_k', ATTN_W), ('attn_v', ATTN_W), ('gates', N_BRANCH * D_MODEL))
IN_COLS = 2 * C_CONV + D_INNER + CONV_DIM + H_SSM + 3 * ATTN_W + N_BRANCH * D_MODEL
BRANCH_ROWS = C_CONV + D_INNER + ATTN_W

kernel_name = 'hybrid_conformer_ssd_diffattn_step'


def in_segment(h, w_in, name):
    start = 0
    for seg_name, size in IN_SEGMENTS:
        if seg_name == name:
            return h @ lax.slice_in_dim(w_in, start, start + size, axis=1)
        start += size
    raise KeyError(name)


def rmsnorm(x, g):
    xf = x.astype(jnp.float32)
    y = xf * lax.rsqrt(jnp.mean(xf * xf, axis=-1, keepdims=True) + EPS)
    return (y * g.astype(jnp.float32)).astype(x.dtype)


def layernorm(x, g, b):
    xf = x.astype(jnp.float32)
    mu = jnp.mean(xf, axis=-1, keepdims=True)
    xc = xf - mu
    y = xc * lax.rsqrt(jnp.mean(xc * xc, axis=-1, keepdims=True) + EPS)
    return (y * g.astype(jnp.float32) + b.astype(jnp.float32)).astype(x.dtype)


def causal_dwconv(buf, x, w, b):
    full = jnp.concatenate([buf.astype(x.dtype), x], axis=1)
    y = lax.conv_general_dilated(full, w.astype(x.dtype)[:, None, :], (1,), 'VALID', dimension_numbers=('NWC', 'WIO', 'NWC'), feature_group_count=x.shape[-1])
    return y + b.astype(x.dtype), full[:, full.shape[1] - (w.shape[0] - 1):]


def segsum(a):
    cs = jnp.cumsum(a, axis=-1)
    t = a.shape[-1]
    diff = cs[..., :, None] - cs[..., None, :]
    return jnp.where(jnp.tril(jnp.ones((t, t), dtype=bool)), diff, -jnp.inf)


def ssd_scan(x, dt, a_neg, bmat, cmat, init_state, chunk):
    f32 = jnp.float32
    n, l, h, p = x.shape
    g, s = bmat.shape[2], bmat.shape[3]
    r = h // g
    nc = l // chunk
    xdt = (x.astype(f32) * dt[..., None]).reshape(n, nc, chunk, g, r, p)
    a = (dt * a_neg).reshape(n, nc, chunk, g, r).transpose(0, 3, 4, 1, 2)
    bc = bmat.astype(f32).reshape(n, nc, chunk, g, s)
    cc = cmat.astype(f32).reshape(n, nc, chunk, g, s)
    a_cs = jnp.cumsum(a, axis=-1)
    cb = jnp.einsum('ncqgs,nckgs->ngcqk', cc, bc)
    w_intra = cb[:, :, None] * jnp.exp(segsum(a))
    y_diag = jnp.einsum('ngrcqk,nckgrp->ncqgrp', w_intra, xdt)
    decay_to_end = jnp.exp(a_cs[..., -1:] - a_cs)
    chunk_states = jnp.einsum('nckgs,ngrck,nckgrp->cngrps', bc, decay_to_end, xdt)
    chunk_decay = jnp.exp(a_cs[..., -1]).transpose(3, 0, 1, 2)

    def step(state, inp):
        contrib, decay = inp
        return state * decay[..., None, None] + contrib, state

    init = init_state.astype(f32).reshape(n, g, r, p, s)
    final, prev = lax.scan(step, init, (chunk_states, chunk_decay))
    y_off = jnp.einsum('ncqgs,cngrps,ngrcq->ncqgrp', cc, prev, jnp.exp(a_cs))
    return (y_diag + y_off).reshape(n, l, h, p), final.reshape(n, h, p, s)


def conformer_conv_branch(u, conv_buf, dw_w, dw_b, ln_g, ln_b):
    a, b = jnp.split(u, 2, axis=-1)
    glu = a * jax.nn.sigmoid(b)
    y, new_buf = causal_dwconv(conv_buf, glu, dw_w, dw_b)
    return jax.nn.silu(layernorm(y, ln_g, ln_b)), new_buf


def mamba2_branch(z, xbc, dt_raw, conv_buf, ssm_state, conv_w, conv_b, dt_bias, a_log, d_skip, norm_g, front_pad, chunk):
    f32 = jnp.float32
    n, l, _ = xbc.shape
    xbc_c, new_buf = causal_dwconv(conv_buf, xbc, conv_w, conv_b)
    xbc_c = jax.nn.silu(xbc_c)
    gs = N_GROUPS * D_STATE
    xs = xbc_c[..., :D_INNER].reshape(n, l, H_SSM, SSM_HEAD_DIM)
    bm = xbc_c[..., D_INNER:D_INNER + gs].reshape(n, l, N_GROUPS, D_STATE)
    cm = xbc_c[..., D_INNER + gs:].reshape(n, l, N_GROUPS, D_STATE)
    dt = jax.nn.softplus(dt_raw.astype(f32) + dt_bias.astype(f32))
    a_neg = -jnp.exp(a_log.astype(f32))
    pad = lambda t: jnp.pad(t, [(0, 0), (front_pad, 0)] + [(0, 0)] * (t.ndim - 2))
    y, final = ssd_scan(pad(xs), pad(dt), a_neg, pad(bm), pad(cm), ssm_state, chunk)
    y = y[:, front_pad:] + xs.astype(f32) * d_skip.astype(f32)[:, None]
    y = y.reshape(n, l, D_INNER) * jax.nn.silu(z.astype(f32))
    yg = y.reshape(n, l, N_GROUPS, D_INNER // N_GROUPS)
    yg = yg * lax.rsqrt(jnp.mean(yg * yg, axis=-1, keepdims=True) + EPS)
    y = yg.reshape(n, l, D_INNER) * norm_g.astype(f32)
    return y.astype(z.dtype), new_buf, final.astype(ssm_state.dtype)


def diff_attn_core(q, k, v, q_pos, k_pos, lam):
    s = jnp.einsum('nqhcd,nkhcd->nhcqk', q, k).astype(jnp.float32) * (HEAD_DIM ** -0.5)
    s = jnp.where(k_pos[None, :] <= q_pos[:, None], s, -jnp.inf)
    p = jax.nn.softmax(s, axis=-1)
    w = p[:, :, 0] - lam * p[:, :, 1]
    return jnp.einsum('nhqk,nkhe->nqhe', w.astype(v.dtype), v)


def prompt_attention(q, k, v, lam):
    n, l = q.shape[:2]
    nb = -(-l // Q_BLOCK)
    qp = jnp.pad(q, [(0, 0), (0, nb * Q_BLOCK - l), (0, 0), (0, 0), (0, 0)])
    k_pos = jnp.arange(l)

    def block(i):
        start = i * Q_BLOCK
        qb = lax.dynamic_slice_in_dim(qp, start, Q_BLOCK, axis=1)
        return diff_attn_core(qb, k, v, start + jnp.arange(Q_BLOCK), k_pos, lam)

    out = lax.map(block, jnp.arange(nb))
    out = jnp.moveaxis(out, 0, 1).reshape(n, nb * Q_BLOCK, ATTN_HEADS, 2 * HEAD_DIM)
    return out[:, :l]


def sample_attention(q, k, v, lam, cache_k, cache_v, page_table, layer):
    s_new = q.shape[1]
    past = page_table.shape[1] * PAGE_SIZE
    q_pos = past + jnp.arange(s_new)
    k_pos = jnp.arange(past + s_new)

    def one(args):
        pages, qi, ki, vi = args
        kp = cache_k[layer, pages].reshape(past, ATTN_HEADS, 2, HEAD_DIM).astype(ki.dtype)
        vp = cache_v[layer, pages].reshape(past, ATTN_HEADS, 2 * HEAD_DIM).astype(vi.dtype)
        kk = jnp.concatenate([kp, ki], axis=0)
        vv = jnp.concatenate([vp, vi], axis=0)
        return diff_attn_core(qi[None], kk[None], vv[None], q_pos, k_pos, lam)[0]

    return lax.map(one, (page_table, q, k, v))


def hybrid_layer(x, layer, lw, conv_buf, ssm_conv_buf, ssm_state, attend, front_pad, chunk):
    n, l, _ = x.shape
    h = rmsnorm(x, lw['norm1_g'])
    seg = lambda name: in_segment(h, lw['w_in'], name)
    yc, new_conv_buf = conformer_conv_branch(seg('conv_glu'), conv_buf, lw['conv_dw_w'], lw['conv_dw_b'], lw['conv_ln_g'], lw['conv_ln_b'])
    ym, new_ssm_conv_buf, new_ssm_state = mamba2_branch(seg('ssm_z'), seg('ssm_xbc'), seg('ssm_dt'), ssm_conv_buf, ssm_state, lw['ssm_conv_w'], lw['ssm_conv_b'], lw['ssm_dt_bias'], lw['ssm_a_log'], lw['ssm_d'], lw['ssm_norm_g'], front_pad, chunk)
    q = seg('attn_q').reshape(n, l, ATTN_HEADS, 2, HEAD_DIM)
    k = seg('attn_k').reshape(n, l, ATTN_HEADS, 2, HEAD_DIM)
    v = seg('attn_v').reshape(n, l, ATTN_HEADS, 2 * HEAD_DIM)
    lam_init = 0.8 - 0.6 * math.exp(-0.3 * layer)
    lv = lw['attn_lambda'].astype(jnp.float32)
    lam = jnp.exp(jnp.sum(lv[0] * lv[1])) - jnp.exp(jnp.sum(lv[2] * lv[3])) + lam_init
    a = attend(q, k, v, lam)
    ya = (rmsnorm(a, lw['attn_subln_g']) * (1.0 - lam_init)).reshape(n, l, ATTN_W)
    wb = lw['w_branch']
    pc = yc @ wb[:C_CONV]
    pm = ym @ wb[C_CONV:C_CONV + D_INNER]
    pa = ya @ wb[C_CONV + D_INNER:]
    g = jax.nn.sigmoid(seg('gates') + lw['b_gate']).reshape(n, l, N_BRANCH, D_MODEL)
    merged = g[..., 0, :] * pc + g[..., 1, :] * pm + g[..., 2, :] * pa
    x = x + merged @ lw['w_o']
    h2 = rmsnorm(x, lw['norm2_g'])
    x = x + jnp.square(jax.nn.relu(h2 @ lw['w_up'])) @ lw['w_down']
    return x, new_conv_buf, new_ssm_conv_buf, new_ssm_state, k.reshape(n, l, ATTN_HEADS, 2 * HEAD_DIM), v


def setup_inputs(seed: int = 0) -> dict:
    key = jax.random.key(seed)
    ks = jax.random.split(key, 32)
    f32 = jnp.float32
    n_pages = PAST_LEN // PAGE_SIZE
    n_pool = (DEC_BATCH * n_pages * 5) // 4
    nrm = lambda k, shape, scale: jax.random.normal(k, shape, f32) * scale
    perm = jax.random.permutation(ks[4], n_pool)
    page_table = perm[:DEC_BATCH * n_pages].reshape(DEC_BATCH, n_pages).astype(jnp.int32)
    dt0 = jnp.exp(jax.random.uniform(ks[15], (DEPTH, H_SSM), f32, math.log(1e-3), math.log(1e-1)))
    return {
        'x_prompt': nrm(ks[0], (BATCH, SEQ, D_MODEL), 1.0),
        'x_sample': nrm(ks[1], (DEC_BATCH, DEC_SEQ, D_MODEL), 1.0),
        'cache_k': nrm(ks[2], (DEPTH, n_pool, PAGE_SIZE, ATTN_HEADS, 2 * HEAD_DIM), 1.0),
        'cache_v': nrm(ks[3], (DEPTH, n_pool, PAGE_SIZE, ATTN_HEADS, 2 * HEAD_DIM), 1.0),
        'page_table': page_table,
        'state_conv': nrm(ks[5], (DEPTH, DEC_BATCH, CONV_K - 1, C_CONV), 0.5),
        'state_ssm_conv': nrm(ks[6], (DEPTH, DEC_BATCH, SSM_CONV_K - 1, CONV_DIM), 1.0),
        'state_ssm': nrm(ks[7], (DEPTH, DEC_BATCH, H_SSM, SSM_HEAD_DIM, D_STATE), 0.1),
        'meta_tokens': nrm(ks[8], (N_META, D_MODEL), 1.0),
        'norm1_g': 1.0 + nrm(ks[9], (DEPTH, D_MODEL), 0.02),
        'w_in': nrm(ks[10], (DEPTH, D_MODEL, IN_COLS), D_MODEL ** -0.5),
        'b_gate': nrm(ks[11], (DEPTH, N_BRANCH * D_MODEL), 0.01),
        'conv_dw_w': nrm(ks[12], (DEPTH, CONV_K, C_CONV), CONV_K ** -0.5),
        'conv_dw_b': nrm(ks[13], (DEPTH, C_CONV), 0.01),
        'conv_ln_g': 1.0 + nrm(ks[14], (DEPTH, C_CONV), 0.02),
        'conv_ln_b': nrm(ks[16], (DEPTH, C_CONV), 0.01),
        'ssm_conv_w': nrm(ks[17], (DEPTH, SSM_CONV_K, CONV_DIM), SSM_CONV_K ** -0.5),
        'ssm_conv_b': nrm(ks[18], (DEPTH, CONV_DIM), 0.01),
        'ssm_dt_bias': dt0 + jnp.log(-jnp.expm1(-dt0)),
        'ssm_a_log': jnp.log(jax.random.uniform(ks[19], (DEPTH, H_SSM), f32, 1.0, 16.0)),
        'ssm_d': 1.0 + nrm(ks[20], (DEPTH, H_SSM), 0.1),
        'ssm_norm_g': 1.0 + nrm(ks[21], (DEPTH, D_INNER), 0.02),
        'attn_lambda': nrm(ks[22], (DEPTH, 4, HEAD_DIM), 0.1),
        'attn_subln_g': 1.0 + nrm(ks[23], (DEPTH, 2 * HEAD_DIM), 0.02),
        'w_branch': nrm(ks[24], (DEPTH, BRANCH_ROWS, D_MODEL), D_MODEL ** -0.5),
        'w_o': nrm(ks[25], (DEPTH, D_MODEL, D_MODEL), D_MODEL ** -0.5),
        'norm2_g': 1.0 + nrm(ks[26], (DEPTH, D_MODEL), 0.02),
        'w_up': nrm(ks[27], (DEPTH, D_MODEL, D_FF), D_MODEL ** -0.5),
        'w_down': nrm(ks[28], (DEPTH, D_FF, D_MODEL), D_FF ** -0.5),
        'final_g': 1.0 + nrm(ks[29], (D_MODEL,), 0.02),
    }


def reference(x_prompt, x_sample, cache_k, cache_v, page_table, state_conv, state_ssm_conv, state_ssm, meta_tokens, norm1_g, w_in, b_gate, conv_dw_w, conv_dw_b, conv_ln_g, conv_ln_b, ssm_conv_w, ssm_conv_b, ssm_dt_bias, ssm_a_log, ssm_d, ssm_norm_g, attn_lambda, attn_subln_g, w_branch, w_o, norm2_g, w_up, w_down, final_g):
    nb = x_prompt.shape[0]
    xp = jnp.concatenate([jnp.broadcast_to(meta_tokens.astype(x_prompt.dtype)[None], (nb, N_META, D_MODEL)), x_prompt], axis=1)
    xs = x_sample
    front_pad = (-N_META) % SSD_CHUNK
    outs_p, outs_s = [], []
    for layer in range(DEPTH):
        lw = {'norm1_g': norm1_g[layer], 'w_in': w_in[layer], 'b_gate': b_gate[layer], 'conv_dw_w': conv_dw_w[layer], 'conv_dw_b': conv_dw_b[layer], 'conv_ln_g': conv_ln_g[layer], 'conv_ln_b': conv_ln_b[layer], 'ssm_conv_w': ssm_conv_w[layer], 'ssm_conv_b': ssm_conv_b[layer], 'ssm_dt_bias': ssm_dt_bias[layer], 'ssm_a_log': ssm_a_log[layer], 'ssm_d': ssm_d[layer], 'ssm_norm_g': ssm_norm_g[layer], 'attn_lambda': attn_lambda[layer], 'attn_subln_g': attn_subln_g[layer], 'w_branch': w_branch[layer], 'w_o': w_o[layer], 'norm2_g': norm2_g[layer], 'w_up': w_up[layer], 'w_down': w_down[layer]}
        xp, *sp = hybrid_layer(xp, layer, lw, jnp.zeros((nb, CONV_K - 1, C_CONV), xp.dtype), jnp.zeros((nb, SSM_CONV_K - 1, CONV_DIM), xp.dtype), jnp.zeros((nb, H_SSM, SSM_HEAD_DIM, D_STATE), xp.dtype), prompt_attention, front_pad, SSD_CHUNK)
        attend_s = functools.partial(sample_attention, cache_k=cache_k, cache_v=cache_v, page_table=page_table, layer=layer)
        xs, *ss = hybrid_layer(xs, layer, lw, state_conv[layer], state_ssm_conv[layer], state_ssm[layer], attend_s, 0, xs.shape[1])
        outs_p.append(sp)
        outs_s.append(ss)
    stk = lambda outs, i: jnp.stack([o[i] for o in outs], axis=0)
    y_prompt = rmsnorm(xp, final_g)[:, N_META:]
    y_sample = rmsnorm(xs, final_g)
    return (y_prompt, y_sample, stk(outs_p, 0), stk(outs_p, 1), stk(outs_p, 2), stk(outs_p, 3), stk(outs_p, 4), stk(outs_s, 0), stk(outs_s, 1), stk(outs_s, 2), stk(outs_s, 3), stk(outs_s, 4))
```

```python
import functools
import math

import jax
import jax.numpy as jnp
from jax import lax
from jax.experimental import pallas as pl
from jax.experimental.pallas import tpu as pltpu

F32 = jnp.float32
BF16 = jnp.bfloat16

D_MODEL = 1024
DEPTH = 2
PAGE_SIZE = 128
N_META = 16
C_CONV = D_MODEL
CONV_K = 31
D_INNER = 2 * D_MODEL
SSM_HEAD_DIM = 64
H_SSM = D_INNER // SSM_HEAD_DIM
N_GROUPS = 8
D_STATE = 128
SSM_CONV_K = 4
CONV_DIM = D_INNER + 2 * N_GROUPS * D_STATE
SSD_CHUNK = 128
HEAD_DIM = 64
ATTN_HEADS = D_MODEL // (2 * HEAD_DIM)
ATTN_W = ATTN_HEADS * 2 * HEAD_DIM
D_FF = 4 * D_MODEL
N_BRANCH = 3
EPS = 1e-6

LANES = 128
SUBLANES = 8
VMEM_LIMIT_BYTES = 56 * 1024 * 1024

ROW_TILE = 512
CONV_TL = 128
CONV_HIST = 32
FLASH_TQ = 256
FLASH_TK = 512
PAGES_PER_STEP = 4
FF_CHUNK = 1024


def _cparams(*sem):
    return pltpu.CompilerParams(dimension_semantics=sem, vmem_limit_bytes=VMEM_LIMIT_BYTES)


def _resident(shape):
    nd = len(shape)
    return pl.BlockSpec(shape, lambda *_: (0,) * nd, pipeline_mode=pl.Buffered(1))


def _rms(x, g):
    return x * lax.rsqrt(jnp.mean(x * x, axis=-1, keepdims=True) + EPS) * g


def _dot(a, b):
    return jnp.dot(a, b, preferred_element_type=F32)


def _dot_nt(a, b):
    return lax.dot_general(a, b, (((1,), (1,)), ((), ())), preferred_element_type=F32)


def _div_pow2(x, n):
    assert n & (n - 1) == 0
    return lax.shift_right_logical(x, n.bit_length() - 1)


def _row_tile(rows):
    return ROW_TILE if rows % ROW_TILE == 0 else rows


def _rms_kernel(x_ref, g_ref, h_ref):
    h_ref[...] = _rms(x_ref[...], g_ref[...]).astype(h_ref.dtype)


def _rms_call(x, g):
    rows, d = x.shape
    tm = _row_tile(rows)
    return pl.pallas_call(
        _rms_kernel,
        grid=(rows // tm,),
        in_specs=[pl.BlockSpec((tm, d), lambda i: (i, 0)), _resident((1, d))],
        out_specs=pl.BlockSpec((tm, d), lambda i: (i, 0)),
        out_shape=jax.ShapeDtypeStruct((rows, d), BF16),
        compiler_params=_cparams("parallel"),
        name="rms0",
    )(x, g.reshape(1, d))


def _proj1_kernel(h_ref, wa_ref, wb_ref, wz_ref, wdt_ref, dtb_ref, alog_ref,
                  glu_ref, zs_ref, dt_ref, a_ref, *, n_heads):
    h = h_ref[...]
    a = _dot(h, wa_ref[...])
    b = _dot(h, wb_ref[...])
    glu_ref[...] = a * jax.nn.sigmoid(b)
    z = _dot(h, wz_ref[...])
    zs_ref[...] = (z * jax.nn.sigmoid(z)).astype(zs_ref.dtype)
    raw = _dot(h, wdt_ref[...]) + dtb_ref[...]
    dt = jnp.maximum(raw, 0.0) + jnp.log1p(jnp.exp(-jnp.abs(raw)))
    lane = lax.broadcasted_iota(jnp.int32, dt.shape, 1)
    dt = jnp.where(lane < n_heads, dt, 0.0)
    dt_ref[...] = dt
    a_ref[...] = dt * (-jnp.exp(alog_ref[...]))


def _proj1_call(h, w):
    rows, d = h.shape
    tm = _row_tile(rows)
    c, di = C_CONV, D_INNER
    row = lambda n: pl.BlockSpec((tm, n), lambda i: (i, 0))
    return pl.pallas_call(
        functools.partial(_proj1_kernel, n_heads=H_SSM),
        grid=(rows // tm,),
        in_specs=[row(d), _resident((d, c)), _resident((d, c)), _resident((d, di)),
                  _resident((d, LANES)), _resident((1, LANES)), _resident((1, LANES))],
        out_specs=[row(c), row(di), row(LANES), row(LANES)],
        out_shape=[jax.ShapeDtypeStruct((rows, c), F32), jax.ShapeDtypeStruct((rows, di), BF16),
                   jax.ShapeDtypeStruct((rows, LANES), F32), jax.ShapeDtypeStruct((rows, LANES), F32)],
        compiler_params=_cparams("parallel"),
        name="proj1",
    )(h, w["wa"], w["wb"], w["wz"], w["wdt"], w["dtb"], w["alog"])


def _proj2_kernel(h_ref, w_ref, o_ref, *, col_chunk):
    h = h_ref[...]
    for c0 in range(0, o_ref.shape[1], col_chunk):
        o_ref[:, c0:c0 + col_chunk] = _dot(h, w_ref[:, c0:c0 + col_chunk]).astype(o_ref.dtype)


def _proj2_call(h, w):
    rows, d = h.shape
    tm = _row_tile(rows)
    cd = CONV_DIM
    return pl.pallas_call(
        functools.partial(_proj2_kernel, col_chunk=min(1024, cd)),
        grid=(rows // tm,),
        in_specs=[pl.BlockSpec((tm, d), lambda i: (i, 0)), _resident((d, cd))],
        out_specs=pl.BlockSpec((tm, cd), lambda i: (i, 0)),
        out_shape=jax.ShapeDtypeStruct((rows, cd), BF16),
        compiler_params=_cparams("parallel"),
        name="proj2",
    )(h, w["wxbc"])


def _proj3_kernel(h_ref, wq_ref, wk_ref, wv_ref, wg_ref, bg_ref,
                  q_ref, k32_ref, v32_ref, kb_ref, vb_ref, g_ref, *, scale, col_chunk):
    h = h_ref[...]
    q_ref[...] = (_dot(h, wq_ref[...]) * scale).astype(q_ref.dtype)
    k = _dot(h, wk_ref[...])
    k32_ref[...] = k
    kb_ref[...] = k.astype(kb_ref.dtype)
    v = _dot(h, wv_ref[...])
    v32_ref[...] = v
    vb_ref[...] = v.astype(vb_ref.dtype)
    for c0 in range(0, g_ref.shape[1], col_chunk):
        g = _dot(h, wg_ref[:, c0:c0 + col_chunk]) + bg_ref[:, c0:c0 + col_chunk]
        g_ref[:, c0:c0 + col_chunk] = jax.nn.sigmoid(g).astype(g_ref.dtype)


def _proj3_call(h, w):
    rows, d = h.shape
    tm = _row_tile(rows)
    aw, gw = ATTN_W, N_BRANCH * D_MODEL
    row = lambda n: pl.BlockSpec((tm, n), lambda i: (i, 0))
    sd = lambda n, dt: jax.ShapeDtypeStruct((rows, n), dt)
    return pl.pallas_call(
        functools.partial(_proj3_kernel, scale=HEAD_DIM ** -0.5, col_chunk=D_MODEL),
        grid=(rows // tm,),
        in_specs=[row(d), _resident((d, aw)), _resident((d, aw)), _resident((d, aw)),
                  _resident((d, gw)), _resident((1, gw))],
        out_specs=[row(aw), row(aw), row(aw), row(aw), row(aw), row(gw)],
        out_shape=[sd(aw, BF16), sd(aw, F32), sd(aw, F32), sd(aw, BF16), sd(aw, BF16), sd(gw, BF16)],
        compiler_params=_cparams("parallel"),
        name="proj3",
    )(h, w["wq"], w["wk"], w["wv"], w["wg"], w["bg"])


def _merge_kernel(yc_ref, ym_ref, ya_ref, g_ref, x_ref, wbc_ref, wbm_ref, wba_ref, wo_ref, n2_ref,
                  x1_ref, h2_ref):
    d = x_ref.shape[1]
    pc = _dot(yc_ref[...].astype(BF16), wbc_ref[...])
    pm = _dot(ym_ref[...].astype(BF16), wbm_ref[...])
    pa = _dot(ya_ref[...].astype(BF16), wba_ref[...])
    merged = (g_ref[:, 0:d].astype(F32) * pc + g_ref[:, d:2 * d].astype(F32) * pm
              + g_ref[:, 2 * d:3 * d].astype(F32) * pa)
    x1 = x_ref[...] + _dot(merged.astype(BF16), wo_ref[...])
    x1_ref[...] = x1
    h2_ref[...] = _rms(x1, n2_ref[...]).astype(h2_ref.dtype)


def _merge_call(yc, ym, ya, g, x, w):
    rows, d = x.shape
    tm = _row_tile(rows)
    c, di, aw = C_CONV, D_INNER, ATTN_W
    row = lambda n: pl.BlockSpec((tm, n), lambda i: (i, 0))
    return pl.pallas_call(
        _merge_kernel,
        grid=(rows // tm,),
        in_specs=[row(c), row(di), row(aw), row(N_BRANCH * d), row(d),
                  _resident((c, d)), _resident((di, d)), _resident((aw, d)), _resident((d, d)),
                  _resident((1, d))],
        out_specs=[row(d), row(d)],
        out_shape=[jax.ShapeDtypeStruct((rows, d), F32), jax.ShapeDtypeStruct((rows, d), BF16)],
        compiler_params=_cparams("parallel"),
        name="merge",
    )(yc, ym, ya, g, x, w["wbc"], w["wbm"], w["wba"], w["wo"], w["n2"])


def _mlp_kernel(h2_ref, x1_ref, wu_ref, wd_ref, gn_ref, *out_refs, ff_chunk, emit_x):
    h2 = h2_ref[...]
    acc = x1_ref[...]
    for c0 in range(0, wu_ref.shape[1], ff_chunk):
        u = _dot(h2, wu_ref[:, c0:c0 + ff_chunk])
        u = jnp.square(jnp.maximum(u, 0.0)).astype(BF16)
        acc = acc + _dot(u, wd_ref[c0:c0 + ff_chunk, :])
    if emit_x:
        out_refs[0][...] = acc
    out_refs[-1][...] = _rms(acc, gn_ref[...]).astype(out_refs[-1].dtype)


def _mlp_call(h2, x1, w, g_next, last):
    rows, d = x1.shape
    tm = _row_tile(rows)
    ff = D_FF
    row = lambda n: pl.BlockSpec((tm, n), lambda i: (i, 0))
    if last:
        out_specs = [row(d)]
        out_shape = [jax.ShapeDtypeStruct((rows, d), F32)]
    else:
        out_specs = [row(d), row(d)]
        out_shape = [jax.ShapeDtypeStruct((rows, d), F32), jax.ShapeDtypeStruct((rows, d), BF16)]
    return pl.pallas_call(
        functools.partial(_mlp_kernel, ff_chunk=min(FF_CHUNK, ff), emit_x=not last),
        grid=(rows // tm,),
        in_specs=[row(d), row(d), _resident((d, ff)), _resident((ff, d)), _resident((1, d))],
        out_specs=out_specs,
        out_shape=out_shape,
        compiler_params=_cparams("parallel"),
        name="mlp",
    )(h2, x1, w["wu"], w["wd"], g_next.reshape(1, d))


def _conv_kernel(x_ref, st_ref, w_ref, b_ref, g_ref, beta_ref, y_ref, nb_ref, win, cbuf,
                 *, tl, kw, rs, cb):
    t = pl.program_id(1)
    nt = pl.num_programs(1)
    c = x_ref.shape[1]
    hist = CONV_HIST
    off = hist - (kw - 1)

    @pl.when(t == 0)
    def _():
        win[0:off, :] = jnp.zeros((off, c), F32)
        win[off:hist, :] = st_ref[0]

    @pl.when(t > 0)
    def _():
        win[0:hist, :] = win[tl:tl + hist, :]

    win[hist:hist + tl, :] = x_ref[...].astype(F32)

    for r0 in range(0, tl, rs):
        for c0 in range(0, c, cb):
            acc = jnp.broadcast_to(b_ref[:, c0:c0 + cb], (rs, cb))
            for k in range(kw):
                acc = acc + w_ref[k:k + 1, c0:c0 + cb] * win[r0 + off + k:r0 + off + k + rs, c0:c0 + cb]
            cbuf[r0:r0 + rs, c0:c0 + cb] = acc

    for r0 in range(0, tl, rs):
        v = cbuf[r0:r0 + rs, :]
        mu = jnp.mean(v, axis=-1, keepdims=True)
        xc = v - mu
        y = xc * lax.rsqrt(jnp.mean(xc * xc, axis=-1, keepdims=True) + EPS) * g_ref[...] + beta_ref[...]
        y_ref[r0:r0 + rs, :] = (y * jax.nn.sigmoid(y)).astype(y_ref.dtype)

    @pl.when(t == nt - 1)
    def _():
        nb_ref[0] = win[tl + off:tl + hist, :]


def _conv_call(glu, state, w, n, l, out_dtype):
    c = glu.shape[1]
    kw = CONV_K
    tl = CONV_TL if l % CONV_TL == 0 else l
    nt = l // tl
    rs = min(16, tl)
    cb = min(512, c)
    st_map = (lambda i, t: (i, 0, 0)) if state.shape[0] == n else (lambda i, t: (0, 0, 0))
    return pl.pallas_call(
        functools.partial(_conv_kernel, tl=tl, kw=kw, rs=rs, cb=cb),
        grid=(n, nt),
        in_specs=[pl.BlockSpec((tl, c), lambda i, t: (i * nt + t, 0)),
                  pl.BlockSpec((1, kw - 1, c), st_map),
                  _resident((kw, c)), _resident((1, c)), _resident((1, c)), _resident((1, c))],
        out_specs=[pl.BlockSpec((tl, c), lambda i, t: (i * nt + t, 0)),
                   pl.BlockSpec((1, kw - 1, c), lambda i, t: (i, 0, 0))],
        out_shape=[jax.ShapeDtypeStruct((n * l, c), out_dtype),
                   jax.ShapeDtypeStruct((n, kw - 1, c), F32)],
        scratch_shapes=[pltpu.VMEM((CONV_HIST + tl, c), F32), pltpu.VMEM((tl, c), F32)],
        compiler_params=_cparams("parallel", "arbitrary"),
        name="conv_branch",
    )(glu, state, w["dw_w"], w["dw_b"], w["ln_g"], w["ln_b"])


def _cumsum_rows(x):
    n = x.shape[0]
    row = lax.broadcasted_iota(jnp.int32, x.shape, 0)
    s = 1
    while s < n:
        x = x + jnp.where(row >= s, pltpu.roll(x, s, axis=0), 0.0)
        s *= 2
    return x


def _ssd_kernel(xbc_ref, zs_ref, dt_ref, a_ref, cst_ref, sst_ref, cw_ref, cb_ref, dsk_ref, ng_ref,
                ym_ref, ncs_ref, fst_ref, xw, xc, st,
                *, q, di, ng, ds, hp, p, kc, last_chunk, last_row, rs, ccb):
    c = pl.program_id(1)
    nc = pl.num_programs(1)
    cd = xbc_ref.shape[1]
    gw = hp * p
    pad = SUBLANES

    @pl.when(c == 0)
    def _():
        xw[pad - (kc - 1):pad, :] = cst_ref[0]
        for g in range(ng):
            st[g] = sst_ref[0, g].T

    @pl.when(c > 0)
    def _():
        xw[0:pad, :] = xw[q:q + pad, :]

    xw[pad:pad + q, :] = xbc_ref[...].astype(F32)

    @pl.when(c == last_chunk)
    def _():
        ncs_ref[0] = xw[pad + last_row - (kc - 2):pad + last_row + 1, :]

    for r0 in range(0, q, rs):
        for c0 in range(0, cd, ccb):
            acc = jnp.broadcast_to(cb_ref[:, c0:c0 + ccb], (rs, ccb))
            for k in range(kc):
                o = pad - (kc - 1) + k
                acc = acc + cw_ref[k:k + 1, c0:c0 + ccb] * xw[r0 + o:r0 + o + rs, c0:c0 + ccb]
            xc[r0:r0 + rs, c0:c0 + ccb] = acc * jax.nn.sigmoid(acc)

    dtv = dt_ref[...]
    acs = _cumsum_rows(a_ref[...])
    acs_t = acs.T
    dt_t = dtv.T
    e = jnp.exp(acs)
    dec = jnp.exp(acs[q - 1:q, :])
    wd_t = dt_t * jnp.exp(acs_t[:, q - 1:q] - acs_t)
    ri = lax.broadcasted_iota(jnp.int32, (q, q), 0)
    ci = lax.broadcasted_iota(jnp.int32, (q, q), 1)
    tril = ci <= ri
    lane_head = _div_pow2(lax.broadcasted_iota(jnp.int32, (1, gw), 1), p)

    for g in range(ng):
        b_f = xc[:, di + g * ds:di + (g + 1) * ds]
        c_b = xc[:, di + ng * ds + g * ds:di + ng * ds + (g + 1) * ds].astype(BF16)
        cbm = _dot_nt(c_b, b_f.astype(BF16))
        b_t = b_f.T
        xg = xc[:, g * gw:(g + 1) * gw]
        xgb = xg.astype(BF16)
        yd = jnp.zeros((q, gw), F32)
        sn = jnp.zeros((ds, gw), F32)
        eg = jnp.zeros((q, gw), F32)
        dg = jnp.zeros((1, gw), F32)
        for j in range(hp):
            h = g * hp + j
            mj = lane_head == j
            seg = acs[:, h:h + 1] - acs_t[h:h + 1, :]
            lm = jnp.exp(jnp.where(tril, seg, -jnp.inf))
            wmat = (cbm * lm * dt_t[h:h + 1, :]).astype(BF16)
            xm = jnp.where(mj, xgb, jnp.zeros_like(xgb))
            yd = yd + _dot(wmat, xm)
            ws = (b_t * wd_t[h:h + 1, :]).astype(BF16)
            sn = sn + _dot(ws, xm)
            eg = jnp.where(mj, e[:, h:h + 1], eg)
            dg = jnp.where(mj, dec[:, h:h + 1], dg)
        stg = st[g]
        yo = _dot(c_b, stg.astype(BF16)) * eg
        st[g] = stg * dg + sn
        y = yd + yo + xg * dsk_ref[:, g * gw:(g + 1) * gw]
        y = y * zs_ref[:, g * gw:(g + 1) * gw].astype(F32)
        y = _rms(y, ng_ref[:, g * gw:(g + 1) * gw])
        ym_ref[:, g * gw:(g + 1) * gw] = y.astype(ym_ref.dtype)

    @pl.when(c == nc - 1)
    def _():
        for g in range(ng):
            fst_ref[0, g] = st[g].T


def _ssd_call(xbc, zs, dt, a, conv_state, ssm_state, w, n, l, l_valid, out_dtype):
    q = SSD_CHUNK
    nc = l // q
    cd, di, ng, ds, p = CONV_DIM, D_INNER, N_GROUPS, D_STATE, SSM_HEAD_DIM
    hp = H_SSM // ng
    gw = hp * p
    kc = SSM_CONV_K
    last_chunk, last_row = (l_valid - 1) // q, (l_valid - 1) % q
    assert last_row >= kc - 2 and H_SSM <= LANES
    cs_map = (lambda i, c: (i, 0, 0)) if conv_state.shape[0] == n else (lambda i, c: (0, 0, 0))
    ss_map = (lambda i, c: (i, 0, 0, 0)) if ssm_state.shape[0] == n else (lambda i, c: (0, 0, 0, 0))
    row = lambda w_: pl.BlockSpec((q, w_), lambda i, c: (i * nc + c, 0))
    return pl.pallas_call(
        functools.partial(_ssd_kernel, q=q, di=di, ng=ng, ds=ds, hp=hp, p=p, kc=kc,
                          last_chunk=last_chunk, last_row=last_row, rs=32, ccb=min(512, cd)),
        grid=(n, nc),
        in_specs=[row(cd), row(di), row(LANES), row(LANES),
                  pl.BlockSpec((1, kc - 1, cd), cs_map),
                  pl.BlockSpec((1, ng, gw, ds), ss_map),
                  _resident((kc, cd)), _resident((1, cd)), _resident((1, di)), _resident((1, di))],
        out_specs=[row(di),
                   pl.BlockSpec((1, kc - 1, cd), lambda i, c: (i, 0, 0)),
                   pl.BlockSpec((1, ng, gw, ds), lambda i, c: (i, 0, 0, 0))],
        out_shape=[jax.ShapeDtypeStruct((n * l, di), out_dtype),
                   jax.ShapeDtypeStruct((n, kc - 1, cd), F32),
                   jax.ShapeDtypeStruct((n, ng, gw, ds), F32)],
        scratch_shapes=[pltpu.VMEM((SUBLANES + q, cd), F32), pltpu.VMEM((q, cd), F32),
                        pltpu.VMEM((ng, ds, gw), F32)],
        compiler_params=_cparams("parallel", "arbitrary"),
        name="ssd",
    )(xbc, zs, dt, a, conv_state, ssm_state, w["cw"], w["cb"], w["dsk"], w["ng"])


def _lambda(lam_ref, lam_init):
    lv = lam_ref[...]
    s01 = jnp.sum(lv[0:1, :] * lv[1:2, :], axis=1, keepdims=True)
    s23 = jnp.sum(lv[2:3, :] * lv[3:4, :], axis=1, keepdims=True)
    return jnp.exp(s01) - jnp.exp(s23) + lam_init


def _softmax_step(s, vb, acc, m_sc, l_sc):
    m_prev = m_sc[...]
    m_new = jnp.maximum(m_prev, jnp.max(s, axis=1, keepdims=True))
    alpha = jnp.exp(m_prev - m_new)
    pr = jnp.exp(s - m_new)
    l_sc[...] = alpha * l_sc[...] + jnp.sum(pr, axis=1, keepdims=True)
    acc[...] = alpha * acc[...] + _dot(pr.astype(BF16), vb)
    m_sc[...] = m_new


def _flash_kernel(q_ref, k_ref, v_ref, mk_ref, mv_ref, lam_ref, g_ref, o_ref, acc, m_sc, l_sc,
                  *, tq, tk, n_prefix, lam_init):
    i = pl.program_id(2)
    hd = q_ref.shape[1] // 2
    qv = q_ref[...]
    lane = lax.broadcasted_iota(jnp.int32, qv.shape, 1)
    zero = jnp.zeros_like(qv)
    q2 = jnp.concatenate([jnp.where(lane < hd, qv, zero), jnp.where(lane >= hd, qv, zero)], axis=0)

    m_sc[...] = jnp.full(m_sc.shape, -jnp.inf, F32)
    l_sc[...] = jnp.zeros(l_sc.shape, F32)
    acc[...] = jnp.zeros(acc.shape, F32)

    s = _dot_nt(q2, mk_ref[...])
    pc = lax.broadcasted_iota(jnp.int32, s.shape, 1)
    _softmax_step(jnp.where(pc < n_prefix, s, -jnp.inf), mv_ref[...], acc, m_sc, l_sc)

    n_full = (i * tq) // tk

    def body(c, carry):
        r0 = pl.multiple_of(c * tk, tk)
        _softmax_step(_dot_nt(q2, k_ref[pl.ds(r0, tk), :]), v_ref[pl.ds(r0, tk), :], acc, m_sc, l_sc)
        return carry

    lax.fori_loop(0, n_full, body, 0)

    r0 = pl.multiple_of(n_full * tk, tk)
    s = _dot_nt(q2, k_ref[pl.ds(r0, tk), :])
    row = lax.broadcasted_iota(jnp.int32, s.shape, 0)
    row = jnp.where(row >= tq, row - tq, row) + i * tq
    col = lax.broadcasted_iota(jnp.int32, s.shape, 1) + n_full * tk
    _softmax_step(jnp.where(col <= row, s, -jnp.inf), v_ref[pl.ds(r0, tk), :], acc, m_sc, l_sc)

    lam = _lambda(lam_ref, lam_init)
    a = acc[...] / l_sc[...]
    o = a[0:tq, :] - lam * a[tq:2 * tq, :]
    o_ref[...] = (_rms(o, g_ref[...]) * (1.0 - lam_init)).astype(o_ref.dtype)


def _flash_call(qb, kb, vb, mk, mv, w, n, l, lam_init):
    hw = 2 * HEAD_DIM
    nh = ATTN_HEADS
    tq = FLASH_TQ if l % FLASH_TQ == 0 else l
    tk = FLASH_TK if l % FLASH_TK == 0 else l
    assert tk % tq == 0
    nq = l // tq
    pk = mk.shape[0]
    return pl.pallas_call(
        functools.partial(_flash_kernel, tq=tq, tk=tk, n_prefix=N_META, lam_init=lam_init),
        grid=(n, nh, nq),
        in_specs=[pl.BlockSpec((tq, hw), lambda b, h, i: (b * nq + i, h)),
                  pl.BlockSpec((l, hw), lambda b, h, i: (b, h)),
                  pl.BlockSpec((l, hw), lambda b, h, i: (b, h)),
                  pl.BlockSpec((pk, hw), lambda b, h, i: (0, h)),
                  pl.BlockSpec((pk, hw), lambda b, h, i: (0, h)),
                  _resident((4, HEAD_DIM)), _resident((1, hw))],
        out_specs=pl.BlockSpec((tq, hw), lambda b, h, i: (b * nq + i, h)),
        out_shape=jax.ShapeDtypeStruct((n * l, nh * hw), BF16),
        scratch_shapes=[pltpu.VMEM((2 * tq, hw), F32), pltpu.VMEM((2 * tq, 1), F32),
                        pltpu.VMEM((2 * tq, 1), F32)],
        compiler_params=_cparams("parallel", "parallel", "arbitrary"),
        name="flash_attn",
    )(qb, kb, vb, mk, mv, w["lam"], w["subln"])


def _short_attn_kernel(*refs, l, nh, n_pages, g_pages, page, lam_init):
    if n_pages:
        pt_ref, q_ref, kn_ref, vn_ref = refs[:4]
        kp_refs = refs[4:4 + g_pages]
        vp_refs = refs[4 + g_pages:4 + 2 * g_pages]
        rest = refs[4 + 2 * g_pages:]
    else:
        q_ref, kn_ref, vn_ref = refs[:3]
        kp_refs = vp_refs = ()
        rest = refs[3:]
    lam_ref, g_ref, o_ref, qbd, acc, m_sc, l_sc, kpad, vpad = rest
    step = pl.program_id(1)
    n_steps = n_pages // g_pages if n_pages else 0
    hd = HEAD_DIM
    hw = 2 * hd
    w = nh * hw
    nhc = 2 * nh
    rows = nhc * l

    @pl.when(step == 0)
    def _():
        qv = q_ref[0]
        qrep = jnp.concatenate([qv] * nhc, axis=0)
        r_hc = _div_pow2(lax.broadcasted_iota(jnp.int32, (rows, w), 0), l)
        l_hc = _div_pow2(lax.broadcasted_iota(jnp.int32, (rows, w), 1), hd)
        qbd[...] = jnp.where(r_hc == l_hc, qrep, 0.0).astype(BF16)
        m_sc[...] = jnp.full(m_sc.shape, -jnp.inf, F32)
        l_sc[...] = jnp.zeros(l_sc.shape, F32)
        acc[...] = jnp.zeros(acc.shape, F32)

    def update(s, vb):
        m_prev = m_sc[...]
        m_new = jnp.maximum(m_prev, jnp.max(s, axis=1, keepdims=True))
        alpha = jnp.exp(m_prev - m_new)
        pr = jnp.exp(s - m_new)
        l_sc[...] = alpha * l_sc[...] + jnp.sum(pr, axis=1, keepdims=True)
        res = _dot(pr.astype(BF16), vb)
        for hc in range(nhc):
            h = hc // 2
            acc[hc * l:(hc + 1) * l, :] = (alpha[hc * l:(hc + 1) * l, :] * acc[hc * l:(hc + 1) * l, :]
                                           + res[hc * l:(hc + 1) * l, h * hw:(h + 1) * hw])
        m_sc[...] = m_new

    if n_pages:
        @pl.when(step < n_steps)
        def _():
            for kp_ref, vp_ref in zip(kp_refs, vp_refs):
                kb = kp_ref[...].astype(BF16)
                update(_dot_nt(qbd[...], kb), vp_ref[...].astype(BF16))

    @pl.when(step == n_steps)
    def _():
        kpad[...] = jnp.zeros(kpad.shape, BF16)
        vpad[...] = jnp.zeros(vpad.shape, BF16)
        kpad[0:l, :] = kn_ref[0].astype(BF16)
        vpad[0:l, :] = vn_ref[0].astype(BF16)
        s = _dot_nt(qbd[...], kpad[...])
        t = lax.broadcasted_iota(jnp.int32, s.shape, 0) & (l - 1)
        col = lax.broadcasted_iota(jnp.int32, s.shape, 1)
        update(jnp.where(col <= t, s, -jnp.inf), vpad[...])
        lam = _lambda(lam_ref, lam_init)
        a = acc[...] / l_sc[...]
        for h in range(nh):
            o = a[(2 * h) * l:(2 * h + 1) * l, :] - lam * a[(2 * h + 1) * l:(2 * h + 2) * l, :]
            o_ref[0, :, h * hw:(h + 1) * hw] = _rms(o, g_ref[...]) * (1.0 - lam_init)


def _short_attn_call(q, kn, vn, w, lam_init, cache_k=None, cache_v=None, page_table=None, layer=0):
    n, l, aw = q.shape
    nh = ATTN_HEADS
    hw = 2 * HEAD_DIM
    rows = 2 * nh * l
    page = PAGE_SIZE
    assert l % SUBLANES == 0 and l <= page
    n_pages = 0 if cache_k is None else page_table.shape[1]
    gp = min(PAGES_PER_STEP, n_pages) if n_pages else 0
    assert n_pages == 0 or n_pages % gp == 0
    n_steps = n_pages // gp if n_pages else 0
    seq = lambda b, s, *_: (b, 0, 0)
    in_specs = [pl.BlockSpec((1, l, aw), seq)] * 3
    operands = [q, kn, vn]
    if n_pages:
        def page_map(j):
            def m(b, s, pt):
                return (layer, pt[b, jnp.minimum(s, n_steps - 1) * gp + j], 0, 0)
            return m
        for cache in (cache_k, cache_v):
            for j in range(gp):
                in_specs.append(pl.BlockSpec((None, None, page, aw), page_map(j)))
                operands.append(cache)
    in_specs += [pl.BlockSpec((4, HEAD_DIM), lambda b, s, *_: (0, 0)),
                 pl.BlockSpec((1, hw), lambda b, s, *_: (0, 0))]
    operands += [w["lam"], w["subln"]]
    grid_spec = pltpu.PrefetchScalarGridSpec(
        num_scalar_prefetch=1 if n_pages else 0,
        grid=(n, n_steps + 1),
        in_specs=in_specs,
        out_specs=pl.BlockSpec((1, l, aw), seq),
        scratch_shapes=[pltpu.VMEM((rows, aw), BF16), pltpu.VMEM((rows, hw), F32),
                        pltpu.VMEM((rows, 1), F32), pltpu.VMEM((rows, 1), F32),
                        pltpu.VMEM((page, aw), BF16), pltpu.VMEM((page, aw), BF16)],
    )
    call = pl.pallas_call(
        functools.partial(_short_attn_kernel, l=l, nh=nh, n_pages=n_pages, g_pages=gp, page=page,
                          lam_init=lam_init),
        grid_spec=grid_spec,
        out_shape=jax.ShapeDtypeStruct((n, l, aw), F32),
        compiler_params=_cparams("parallel", "arbitrary"),
        name="short_attn",
    )
    return call(page_table, *operands) if n_pages else call(*operands)


def _layer_weights(layer, p):
    d, c, di, cd, aw = D_MODEL, C_CONV, D_INNER, CONV_DIM, ATTN_W
    w_in = p["w_in"][layer]
    o = 0
    cols = {}
    for name, size in (("glu", 2 * c), ("z", di), ("xbc", cd), ("dt", H_SSM), ("q", aw), ("k", aw),
                       ("v", aw), ("g", N_BRANCH * d)):
        cols[name] = w_in[:, o:o + size]
        o += size
    wb = p["w_branch"][layer]
    row = lambda v: v.astype(F32).reshape(1, -1)
    pad_lanes = lambda v: jnp.pad(v, ((0, 0), (0, LANES - v.shape[1])))
    return {
        "wa": cols["glu"][:, :c].astype(BF16), "wb": cols["glu"][:, c:].astype(BF16),
        "wz": cols["z"].astype(BF16), "wxbc": cols["xbc"].astype(BF16),
        "wdt": pad_lanes(cols["dt"]).astype(BF16),
        "dtb": pad_lanes(row(p["ssm_dt_bias"][layer])), "alog": pad_lanes(row(p["ssm_a_log"][layer])),
        "wq": cols["q"].astype(BF16), "wk": cols["k"].astype(BF16), "wv": cols["v"].astype(BF16),
        "wg": cols["g"].astype(BF16), "bg": row(p["b_gate"][layer]),
        "dw_w": p["conv_dw_w"][layer].astype(F32), "dw_b": row(p["conv_dw_b"][layer]),
        "ln_g": row(p["conv_ln_g"][layer]), "ln_b": row(p["conv_ln_b"][layer]),
        "cw": p["ssm_conv_w"][layer].astype(F32), "cb": row(p["ssm_conv_b"][layer]),
        "dsk": row(jnp.repeat(p["ssm_d"][layer], SSM_HEAD_DIM)), "ng": row(p["ssm_norm_g"][layer]),
        "lam": p["attn_lambda"][layer].astype(F32), "subln": row(p["attn_subln_g"][layer]),
        "wbc": wb[:c].astype(BF16), "wbm": wb[c:c + di].astype(BF16), "wba": wb[c + di:].astype(BF16),
        "wo": p["w_o"][layer].astype(BF16), "n2": row(p["norm2_g"][layer]),
        "wu": p["w_up"][layer].astype(BF16), "wd": p["w_down"][layer].astype(BF16),
    }


def _pad_time(x, n, l, lp):
    w = x.shape[1]
    return jnp.pad(x.reshape(n, l, w), ((0, 0), (0, lp - l), (0, 0))).reshape(n * lp, w)


def _short_group(pr, r0, n, l, w, lam_init, conv_state, sconv_state, ssm_state, attn_kwargs):
    r1 = r0 + n * l
    q = SSD_CHUNK
    hp = H_SSM // N_GROUPS
    yc, nconv = _conv_call(pr["glu"][r0:r1], conv_state, w, n, l, F32)
    padded = [_pad_time(pr[name][r0:r1], n, l, q) for name in ("xbc", "zs", "dt", "a")]
    ssm_state = ssm_state.reshape(ssm_state.shape[0], N_GROUPS, hp * SSM_HEAD_DIM, D_STATE)
    ym, nsconv, nssm = _ssd_call(*padded, sconv_state, ssm_state, w, n, q, l, F32)
    ym = ym.reshape(n, q, D_INNER)[:, :l].reshape(n * l, D_INNER)
    seq3 = lambda x: x[r0:r1].astype(F32).reshape(n, l, ATTN_W)
    ya = _short_attn_call(seq3(pr["q"]), seq3(pr["k32"]), seq3(pr["v32"]), w, lam_init, **attn_kwargs)
    return yc, ym, ya.reshape(n * l, ATTN_W), nconv, nsconv, nssm


def _project(h, w):
    glu, zs, dt, a = _proj1_call(h, w)
    xbc = _proj2_call(h, w)
    q, k32, v32, kb, vb, g = _proj3_call(h, w)
    return {"glu": glu, "zs": zs, "dt": dt, "a": a, "xbc": xbc, "q": q, "k32": k32, "v32": v32,
            "kb": kb, "vb": vb, "g": g}


def kernel(x_prompt, x_sample, cache_k, cache_v, page_table, state_conv, state_ssm_conv, state_ssm, meta_tokens, norm1_g, w_in, b_gate, conv_dw_w, conv_dw_b, conv_ln_g, conv_ln_b, ssm_conv_w, ssm_conv_b, ssm_dt_bias, ssm_a_log, ssm_d, ssm_norm_g, attn_lambda, attn_subln_g, w_branch, w_o, norm2_g, w_up, w_down, final_g):
    params = dict(w_in=w_in, b_gate=b_gate, conv_dw_w=conv_dw_w, conv_dw_b=conv_dw_b, conv_ln_g=conv_ln_g,
                  conv_ln_b=conv_ln_b, ssm_conv_w=ssm_conv_w, ssm_conv_b=ssm_conv_b, ssm_dt_bias=ssm_dt_bias,
                  ssm_a_log=ssm_a_log, ssm_d=ssm_d, ssm_norm_g=ssm_norm_g, attn_lambda=attn_lambda,
                  attn_subln_g=attn_subln_g, w_branch=w_branch, w_o=w_o, norm2_g=norm2_g, w_up=w_up,
                  w_down=w_down)
    nb, seq, d = x_prompt.shape
    ns, ls, _ = x_sample.shape
    nm = meta_tokens.shape[0]
    depth = w_in.shape[0]
    hp = H_SSM // N_GROUPS
    gw = hp * SSM_HEAD_DIM
    assert nm == N_META and nm % SUBLANES == 0

    xm = x_prompt.reshape(nb * seq, d)
    xt = jnp.concatenate([x_sample.reshape(ns * ls, d), meta_tokens.astype(x_sample.dtype)], axis=0)
    rs = ns * ls
    cache_k4 = cache_k.reshape(cache_k.shape[0], cache_k.shape[1], PAGE_SIZE, ATTN_W)
    cache_v4 = cache_v.reshape(cache_v.shape[0], cache_v.shape[1], PAGE_SIZE, ATTN_W)

    hm = _rms_call(xm, norm1_g[0])
    ht = _rms_call(xt, norm1_g[0])
    outs_p, outs_s = [], []
    for layer in range(depth):
        w = _layer_weights(layer, params)
        lam_init = 0.8 - 0.6 * math.exp(-0.3 * layer)
        last = layer == depth - 1
        g_next = final_g if last else norm1_g[layer + 1]

        pt = _project(ht, w)
        s_yc, s_ym, s_ya, s_conv, s_sconv, s_ssm = _short_group(
            pt, 0, ns, ls, w, lam_init, state_conv[layer], state_ssm_conv[layer], state_ssm[layer],
            dict(cache_k=cache_k4, cache_v=cache_v4, page_table=page_table, layer=layer))
        m_yc, m_ym, m_ya, m_conv, m_sconv, m_ssm = _short_group(
            pt, rs, 1, nm, w, lam_init,
            jnp.zeros((1, CONV_K - 1, C_CONV), F32), jnp.zeros((1, SSM_CONV_K - 1, CONV_DIM), F32),
            jnp.zeros((1, H_SSM, SSM_HEAD_DIM, D_STATE), F32), {})
        cat = lambda a, b: jnp.concatenate([a, b], axis=0)
        xt1, ht2 = _merge_call(cat(s_yc, m_yc), cat(s_ym, m_ym), cat(s_ya, m_ya), pt["g"], xt, w)
        tail_out = _mlp_call(ht2, xt1, w, g_next, last)

        pm = _project(hm, w)
        p_yc, p_conv = _conv_call(pm["glu"], m_conv, w, nb, seq, BF16)
        p_ym, p_sconv, p_ssm = _ssd_call(pm["xbc"], pm["zs"], pm["dt"], pm["a"], m_sconv, m_ssm, w,
                                         nb, seq, seq, BF16)
        pad_keys = lambda x: jnp.pad(x[rs:rs + nm], ((0, LANES - nm), (0, 0)))
        p_ya = _flash_call(pm["q"], pm["kb"], pm["vb"], pad_keys(pt["kb"]), pad_keys(pt["vb"]), w,
                           nb, seq, lam_init)
        xm1, hm2 = _merge_call(p_yc, p_ym, p_ya, pm["g"], xm, w)
        main_out = _mlp_call(hm2, xm1, w, g_next, last)

        def with_prefix(x_tail, x_main):
            pre = jnp.broadcast_to(x_tail[rs:rs + nm][None], (nb, nm, ATTN_W))
            full = jnp.concatenate([pre, x_main.reshape(nb, seq, ATTN_W)], axis=1)
            return full.reshape(nb, seq + nm, ATTN_HEADS, 2 * HEAD_DIM)

        heads = lambda x: x[:rs].reshape(ns, ls, ATTN_HEADS, 2 * HEAD_DIM)
        state4 = lambda s: s.reshape(s.shape[0], H_SSM, SSM_HEAD_DIM, D_STATE)
        outs_p.append((p_conv, p_sconv, state4(p_ssm), with_prefix(pt["k32"], pm["k32"]),
                       with_prefix(pt["v32"], pm["v32"])))
        outs_s.append((s_conv, s_sconv, state4(s_ssm), heads(pt["k32"]), heads(pt["v32"])))
        if last:
            y_main, y_tail = main_out[0], tail_out[0]
        else:
            (xm, hm), (xt, ht) = main_out, tail_out

    stk = lambda outs, i: jnp.stack([o[i] for o in outs], axis=0)
    y_prompt = y_main.reshape(nb, seq, d)
    y_sample = y_tail[:rs].reshape(ns, ls, d)
    return (y_prompt, y_sample, stk(outs_p, 0), stk(outs_p, 1), stk(outs_p, 2), stk(outs_p, 3), stk(outs_p, 4),
            stk(outs_s, 0), stk(outs_s, 1), stk(outs_s, 2), stk(outs_s, 3), stk(outs_s, 4))
```

```python
import functools
import math

import jax
import jax.numpy as jnp
from jax import lax
from jax.experimental import pallas as pl
from jax.experimental.pallas import tpu as pltpu

F32 = jnp.float32
BF16 = jnp.bfloat16

D_MODEL = 1024
DEPTH = 2
PAGE_SIZE = 128
N_META = 16
C_CONV = D_MODEL
CONV_K = 31
D_INNER = 2 * D_MODEL
SSM_HEAD_DIM = 64
H_SSM = D_INNER // SSM_HEAD_DIM
N_GROUPS = 8
D_STATE = 128
SSM_CONV_K = 4
CONV_DIM = D_INNER + 2 * N_GROUPS * D_STATE
SSD_CHUNK = 128
HEAD_DIM = 64
ATTN_HEADS = D_MODEL // (2 * HEAD_DIM)
ATTN_W = ATTN_HEADS * 2 * HEAD_DIM
D_FF = 4 * D_MODEL
N_BRANCH = 3
EPS = 1e-6

LANES = 128
SUBLANES = 8
VMEM_LIMIT_BYTES = 56 * 1024 * 1024

ROW_TILE = 512
CONV_TL = 128
CONV_HIST = 32
FLASH_TQ = 512
PAGES_PER_STEP = 4
FF_CHUNK = 1024


def _cparams(*sem):
    return pltpu.CompilerParams(dimension_semantics=sem, vmem_limit_bytes=VMEM_LIMIT_BYTES)


def _resident(shape):
    nd = len(shape)
    return pl.BlockSpec(shape, lambda *_: (0,) * nd, pipeline_mode=pl.Buffered(1))


def _rms(x, g):
    return x * lax.rsqrt(jnp.mean(x * x, axis=-1, keepdims=True) + EPS) * g


def _dot(a, b):
    return jnp.dot(a, b, preferred_element_type=F32)


def _dot_nt(a, b):
    return lax.dot_general(a, b, (((1,), (1,)), ((), ())), preferred_element_type=F32)


def _div_pow2(x, n):
    assert n & (n - 1) == 0
    return lax.shift_right_logical(x, n.bit_length() - 1)


def _row_tile(rows):
    return ROW_TILE if rows % ROW_TILE == 0 else rows


def _rms_kernel(x_ref, g_ref, h_ref):
    h_ref[...] = _rms(x_ref[...], g_ref[...]).astype(h_ref.dtype)


def _rms_call(x, g):
    rows, d = x.shape
    tm = _row_tile(rows)
    return pl.pallas_call(
        _rms_kernel,
        grid=(rows // tm,),
        in_specs=[pl.BlockSpec((tm, d), lambda i: (i, 0)), _resident((1, d))],
        out_specs=pl.BlockSpec((tm, d), lambda i: (i, 0)),
        out_shape=jax.ShapeDtypeStruct((rows, d), BF16),
        compiler_params=_cparams("parallel"),
        name="rms0",
    )(x, g.reshape(1, d))


def _proj1_kernel(h_ref, wa_ref, wb_ref, wz_ref, wdt_ref, dtb_ref, alog_ref,
                  glu_ref, zs_ref, dt_ref, a_ref, *, n_heads):
    h = h_ref[...]
    a = _dot(h, wa_ref[...])
    b = _dot(h, wb_ref[...])
    glu_ref[...] = a * jax.nn.sigmoid(b)
    z = _dot(h, wz_ref[...])
    zs_ref[...] = (z * jax.nn.sigmoid(z)).astype(zs_ref.dtype)
    raw = _dot(h, wdt_ref[...]) + dtb_ref[...]
    dt = jnp.maximum(raw, 0.0) + jnp.log1p(jnp.exp(-jnp.abs(raw)))
    lane = lax.broadcasted_iota(jnp.int32, dt.shape, 1)
    dt = jnp.where(lane < n_heads, dt, 0.0)
    dt_ref[...] = dt
    a_ref[...] = dt * (-jnp.exp(alog_ref[...]))


def _proj1_call(h, w):
    rows, d = h.shape
    tm = _row_tile(rows)
    c, di = C_CONV, D_INNER
    row = lambda n: pl.BlockSpec((tm, n), lambda i: (i, 0))
    return pl.pallas_call(
        functools.partial(_proj1_kernel, n_heads=H_SSM),
        grid=(rows // tm,),
        in_specs=[row(d), _resident((d, c)), _resident((d, c)), _resident((d, di)),
                  _resident((d, LANES)), _resident((1, LANES)), _resident((1, LANES))],
        out_specs=[row(c), row(di), row(LANES), row(LANES)],
        out_shape=[jax.ShapeDtypeStruct((rows, c), F32), jax.ShapeDtypeStruct((rows, di), BF16),
                   jax.ShapeDtypeStruct((rows, LANES), F32), jax.ShapeDtypeStruct((rows, LANES), F32)],
        compiler_params=_cparams("parallel"),
        name="proj1",
    )(h, w["wa"], w["wb"], w["wz"], w["wdt"], w["dtb"], w["alog"])


def _proj2_kernel(h_ref, w_ref, o_ref, *, col_chunk):
    h = h_ref[...]
    for c0 in range(0, o_ref.shape[1], col_chunk):
        o_ref[:, c0:c0 + col_chunk] = _dot(h, w_ref[:, c0:c0 + col_chunk]).astype(o_ref.dtype)


def _proj2_call(h, w):
    rows, d = h.shape
    tm = _row_tile(rows)
    cd = CONV_DIM
    return pl.pallas_call(
        functools.partial(_proj2_kernel, col_chunk=min(1024, cd)),
        grid=(rows // tm,),
        in_specs=[pl.BlockSpec((tm, d), lambda i: (i, 0)), _resident((d, cd))],
        out_specs=pl.BlockSpec((tm, cd), lambda i: (i, 0)),
        out_shape=jax.ShapeDtypeStruct((rows, cd), BF16),
        compiler_params=_cparams("parallel"),
        name="proj2",
    )(h, w["wxbc"])


def _proj3_kernel(h_ref, wq_ref, wk_ref, wv_ref, wg_ref, bg_ref,
                  q_ref, k32_ref, v32_ref, kb_ref, vb_ref, g_ref, *, scale, col_chunk):
    h = h_ref[...]
    q_ref[...] = (_dot(h, wq_ref[...]) * scale).astype(q_ref.dtype)
    k = _dot(h, wk_ref[...])
    k32_ref[...] = k
    kb_ref[...] = k.astype(kb_ref.dtype)
    v = _dot(h, wv_ref[...])
    v32_ref[...] = v
    vb_ref[...] = v.astype(vb_ref.dtype)
    for c0 in range(0, g_ref.shape[1], col_chunk):
        g = _dot(h, wg_ref[:, c0:c0 + col_chunk]) + bg_ref[:, c0:c0 + col_chunk]
        g_ref[:, c0:c0 + col_chunk] = jax.nn.sigmoid(g).astype(g_ref.dtype)


def _proj3_call(h, w):
    rows, d = h.shape
    tm = _row_tile(rows)
    aw, gw = ATTN_W, N_BRANCH * D_MODEL
    row = lambda n: pl.BlockSpec((tm, n), lambda i: (i, 0))
    sd = lambda n, dt: jax.ShapeDtypeStruct((rows, n), dt)
    return pl.pallas_call(
        functools.partial(_proj3_kernel, scale=HEAD_DIM ** -0.5, col_chunk=D_MODEL),
        grid=(rows // tm,),
        in_specs=[row(d), _resident((d, aw)), _resident((d, aw)), _resident((d, aw)),
                  _resident((d, gw)), _resident((1, gw))],
        out_specs=[row(aw), row(aw), row(aw), row(aw), row(aw), row(gw)],
        out_shape=[sd(aw, BF16), sd(aw, F32), sd(aw, F32), sd(aw, BF16), sd(aw, BF16), sd(gw, BF16)],
        compiler_params=_cparams("parallel"),
        name="proj3",
    )(h, w["wq"], w["wk"], w["wv"], w["wg"], w["bg"])


def _merge_kernel(yc_ref, ym_ref, ya_ref, g_ref, x_ref, wbc_ref, wbm_ref, wba_ref, wo_ref, n2_ref,
                  x1_ref, h2_ref):
    d = x_ref.shape[1]
    pc = _dot(yc_ref[...].astype(BF16), wbc_ref[...])
    pm = _dot(ym_ref[...].astype(BF16), wbm_ref[...])
    pa = _dot(ya_ref[...].astype(BF16), wba_ref[...])
    merged = (g_ref[:, 0:d].astype(F32) * pc + g_ref[:, d:2 * d].astype(F32) * pm
              + g_ref[:, 2 * d:3 * d].astype(F32) * pa)
    x1 = x_ref[...] + _dot(merged.astype(BF16), wo_ref[...])
    x1_ref[...] = x1
    h2_ref[...] = _rms(x1, n2_ref[...]).astype(h2_ref.dtype)


def _merge_call(yc, ym, ya, g, x, w):
    rows, d = x.shape
    tm = _row_tile(rows)
    c, di, aw = C_CONV, D_INNER, ATTN_W
    row = lambda n: pl.BlockSpec((tm, n), lambda i: (i, 0))
    return pl.pallas_call(
        _merge_kernel,
        grid=(rows // tm,),
        in_specs=[row(c), row(di), row(aw), row(N_BRANCH * d), row(d),
                  _resident((c, d)), _resident((di, d)), _resident((aw, d)), _resident((d, d)),
                  _resident((1, d))],
        out_specs=[row(d), row(d)],
        out_shape=[jax.ShapeDtypeStruct((rows, d), F32), jax.ShapeDtypeStruct((rows, d), BF16)],
        compiler_params=_cparams("parallel"),
        name="merge",
    )(yc, ym, ya, g, x, w["wbc"], w["wbm"], w["wba"], w["wo"], w["n2"])


def _mlp_kernel(h2_ref, x1_ref, wu_ref, wd_ref, gn_ref, *out_refs, ff_chunk, emit_x):
    h2 = h2_ref[...]
    acc = x1_ref[...]
    for c0 in range(0, wu_ref.shape[1], ff_chunk):
        u = _dot(h2, wu_ref[:, c0:c0 + ff_chunk])
        u = jnp.square(jnp.maximum(u, 0.0)).astype(BF16)
        acc = acc + _dot(u, wd_ref[c0:c0 + ff_chunk, :])
    if emit_x:
        out_refs[0][...] = acc
    out_refs[-1][...] = _rms(acc, gn_ref[...]).astype(out_refs[-1].dtype)


def _mlp_call(h2, x1, w, g_next, last):
    rows, d = x1.shape
    tm = _row_tile(rows)
    ff = D_FF
    row = lambda n: pl.BlockSpec((tm, n), lambda i: (i, 0))
    if last:
        out_specs = [row(d)]
        out_shape = [jax.ShapeDtypeStruct((rows, d), F32)]
    else:
        out_specs = [row(d), row(d)]
        out_shape = [jax.ShapeDtypeStruct((rows, d), F32), jax.ShapeDtypeStruct((rows, d), BF16)]
    return pl.pallas_call(
        functools.partial(_mlp_kernel, ff_chunk=min(FF_CHUNK, ff), emit_x=not last),
        grid=(rows // tm,),
        in_specs=[row(d), row(d), _resident((d, ff)), _resident((ff, d)), _resident((1, d))],
        out_specs=out_specs,
        out_shape=out_shape,
        compiler_params=_cparams("parallel"),
        name="mlp",
    )(h2, x1, w["wu"], w["wd"], g_next.reshape(1, d))


def _conv_kernel(x_ref, st_ref, w_ref, b_ref, g_ref, beta_ref, y_ref, nb_ref, win, cbuf, sh,
                 *, tl, kw, rs, cb):
    t = pl.program_id(1)
    nt = pl.num_programs(1)
    c = x_ref.shape[1]
    hist = CONV_HIST
    off = hist - (kw - 1)

    @pl.when(t == 0)
    def _():
        win[0:off, :] = jnp.zeros((off, c), F32)
        win[off:hist, :] = st_ref[0]

    @pl.when(t > 0)
    def _():
        win[0:hist, :] = win[tl:tl + hist, :]

    win[hist:hist + tl, :] = x_ref[...].astype(F32)

    n_sh = hist + tl - SUBLANES
    for s in range(1, SUBLANES):
        sh[s - 1, 0:n_sh, :] = win[s:s + n_sh, :]

    for r0 in range(0, tl, rs):
        for c0 in range(0, c, cb):
            acc = jnp.broadcast_to(b_ref[:, c0:c0 + cb], (rs, cb))
            for k in range(kw):
                s = (off + k) % SUBLANES
                a = r0 + off + k - s
                tap = win[a:a + rs, c0:c0 + cb] if s == 0 else sh[s - 1, a:a + rs, c0:c0 + cb]
                acc = acc + w_ref[k:k + 1, c0:c0 + cb] * tap
            cbuf[r0:r0 + rs, c0:c0 + cb] = acc

    for r0 in range(0, tl, rs):
        v = cbuf[r0:r0 + rs, :]
        mu = jnp.mean(v, axis=-1, keepdims=True)
        xc = v - mu
        y = xc * lax.rsqrt(jnp.mean(xc * xc, axis=-1, keepdims=True) + EPS) * g_ref[...] + beta_ref[...]
        y_ref[r0:r0 + rs, :] = (y * jax.nn.sigmoid(y)).astype(y_ref.dtype)

    @pl.when(t == nt - 1)
    def _():
        nb_ref[0] = win[tl + off:tl + hist, :]


def _conv_call(glu, state, w, n, l, out_dtype):
    c = glu.shape[1]
    kw = CONV_K
    tl = CONV_TL if l % CONV_TL == 0 else l
    nt = l // tl
    rs = min(16, tl)
    cb = min(512, c)
    st_map = (lambda i, t: (i, 0, 0)) if state.shape[0] == n else (lambda i, t: (0, 0, 0))
    return pl.pallas_call(
        functools.partial(_conv_kernel, tl=tl, kw=kw, rs=rs, cb=cb),
        grid=(n, nt),
        in_specs=[pl.BlockSpec((tl, c), lambda i, t: (i * nt + t, 0)),
                  pl.BlockSpec((1, kw - 1, c), st_map),
                  _resident((kw, c)), _resident((1, c)), _resident((1, c)), _resident((1, c))],
        out_specs=[pl.BlockSpec((tl, c), lambda i, t: (i * nt + t, 0)),
                   pl.BlockSpec((1, kw - 1, c), lambda i, t: (i, 0, 0))],
        out_shape=[jax.ShapeDtypeStruct((n * l, c), out_dtype),
                   jax.ShapeDtypeStruct((n, kw - 1, c), F32)],
        scratch_shapes=[pltpu.VMEM((CONV_HIST + tl, c), F32), pltpu.VMEM((tl, c), F32),
                        pltpu.VMEM((SUBLANES - 1, CONV_HIST + tl, c), F32)],
        compiler_params=_cparams("parallel", "arbitrary"),
        name="conv_branch",
    )(glu, state, w["dw_w"], w["dw_b"], w["ln_g"], w["ln_b"])


def _cumsum_rows(x):
    n = x.shape[0]
    row = lax.broadcasted_iota(jnp.int32, x.shape, 0)
    s = 1
    while s < n:
        x = x + jnp.where(row >= s, pltpu.roll(x, s, axis=0), 0.0)
        s *= 2
    return x


def _ssd_kernel(xbc_ref, zs_ref, dt_ref, a_ref, cst_ref, sst_ref, cw_ref, cb_ref, dsk_ref, ng_ref,
                ym_ref, ncs_ref, fst_ref, xw, xc, st,
                *, q, di, ng, ds, hp, p, kc, last_chunk, last_row, rs, ccb):
    c = pl.program_id(1)
    nc = pl.num_programs(1)
    cd = xbc_ref.shape[1]
    gw = hp * p
    pad = SUBLANES

    @pl.when(c == 0)
    def _():
        xw[pad - (kc - 1):pad, :] = cst_ref[0]
        for g in range(ng):
            st[g] = sst_ref[0, g].T

    @pl.when(c > 0)
    def _():
        xw[0:pad, :] = xw[q:q + pad, :]

    xw[pad:pad + q, :] = xbc_ref[...].astype(F32)

    @pl.when(c == last_chunk)
    def _():
        ncs_ref[0] = xw[pad + last_row - (kc - 2):pad + last_row + 1, :]

    for r0 in range(0, q, rs):
        for c0 in range(0, cd, ccb):
            acc = jnp.broadcast_to(cb_ref[:, c0:c0 + ccb], (rs, ccb))
            for k in range(kc):
                o = pad - (kc - 1) + k
                acc = acc + cw_ref[k:k + 1, c0:c0 + ccb] * xw[r0 + o:r0 + o + rs, c0:c0 + ccb]
            xc[r0:r0 + rs, c0:c0 + ccb] = acc * jax.nn.sigmoid(acc)

    dtv = dt_ref[...]
    acs = _cumsum_rows(a_ref[...])
    acs_t = acs.T
    dt_t = dtv.T
    e = jnp.exp(acs)
    dec = jnp.exp(acs[q - 1:q, :])
    wd_t = dt_t * jnp.exp(acs_t[:, q - 1:q] - acs_t)
    ri = lax.broadcasted_iota(jnp.int32, (q, q), 0)
    ci = lax.broadcasted_iota(jnp.int32, (q, q), 1)
    tril = ci <= ri
    lane_head = _div_pow2(lax.broadcasted_iota(jnp.int32, (1, gw), 1), p)

    for g in range(ng):
        b_f = xc[:, di + g * ds:di + (g + 1) * ds]
        c_b = xc[:, di + ng * ds + g * ds:di + ng * ds + (g + 1) * ds].astype(BF16)
        cbm = _dot_nt(c_b, b_f.astype(BF16))
        b_t = b_f.T
        xg = xc[:, g * gw:(g + 1) * gw]
        xgb = xg.astype(BF16)
        yd = jnp.zeros((q, gw), F32)
        sn = jnp.zeros((ds, gw), F32)
        eg = jnp.zeros((q, gw), F32)
        dg = jnp.zeros((1, gw), F32)
        for j in range(hp):
            h = g * hp + j
            mj = lane_head == j
            seg = acs[:, h:h + 1] - acs_t[h:h + 1, :]
            lm = jnp.exp(jnp.where(tril, seg, -jnp.inf))
            wmat = (cbm * lm * dt_t[h:h + 1, :]).astype(BF16)
            xm = jnp.where(mj, xgb, jnp.zeros_like(xgb))
            yd = yd + _dot(wmat, xm)
            ws = (b_t * wd_t[h:h + 1, :]).astype(BF16)
            sn = sn + _dot(ws, xm)
            eg = jnp.where(mj, e[:, h:h + 1], eg)
            dg = jnp.where(mj, dec[:, h:h + 1], dg)
        stg = st[g]
        yo = _dot(c_b, stg.astype(BF16)) * eg
        st[g] = stg * dg + sn
        y = yd + yo + xg * dsk_ref[:, g * gw:(g + 1) * gw]
        y = y * zs_ref[:, g * gw:(g + 1) * gw].astype(F32)
        y = _rms(y, ng_ref[:, g * gw:(g + 1) * gw])
        ym_ref[:, g * gw:(g + 1) * gw] = y.astype(ym_ref.dtype)

    @pl.when(c == nc - 1)
    def _():
        for g in range(ng):
            fst_ref[0, g] = st[g].T


def _ssd_call(xbc, zs, dt, a, conv_state, ssm_state, w, n, l, l_valid, out_dtype):
    q = SSD_CHUNK
    nc = l // q
    cd, di, ng, ds, p = CONV_DIM, D_INNER, N_GROUPS, D_STATE, SSM_HEAD_DIM
    hp = H_SSM // ng
    gw = hp * p
    kc = SSM_CONV_K
    last_chunk, last_row = (l_valid - 1) // q, (l_valid - 1) % q
    assert last_row >= kc - 2 and H_SSM <= LANES
    cs_map = (lambda i, c: (i, 0, 0)) if conv_state.shape[0] == n else (lambda i, c: (0, 0, 0))
    ss_map = (lambda i, c: (i, 0, 0, 0)) if ssm_state.shape[0] == n else (lambda i, c: (0, 0, 0, 0))
    row = lambda w_: pl.BlockSpec((q, w_), lambda i, c: (i * nc + c, 0))
    return pl.pallas_call(
        functools.partial(_ssd_kernel, q=q, di=di, ng=ng, ds=ds, hp=hp, p=p, kc=kc,
                          last_chunk=last_chunk, last_row=last_row, rs=32, ccb=min(512, cd)),
        grid=(n, nc),
        in_specs=[row(cd), row(di), row(LANES), row(LANES),
                  pl.BlockSpec((1, kc - 1, cd), cs_map),
                  pl.BlockSpec((1, ng, gw, ds), ss_map),
                  _resident((kc, cd)), _resident((1, cd)), _resident((1, di)), _resident((1, di))],
        out_specs=[row(di),
                   pl.BlockSpec((1, kc - 1, cd), lambda i, c: (i, 0, 0)),
                   pl.BlockSpec((1, ng, gw, ds), lambda i, c: (i, 0, 0, 0))],
        out_shape=[jax.ShapeDtypeStruct((n * l, di), out_dtype),
                   jax.ShapeDtypeStruct((n, kc - 1, cd), F32),
                   jax.ShapeDtypeStruct((n, ng, gw, ds), F32)],
        scratch_shapes=[pltpu.VMEM((SUBLANES + q, cd), F32), pltpu.VMEM((q, cd), F32),
                        pltpu.VMEM((ng, ds, gw), F32)],
        compiler_params=_cparams("parallel", "arbitrary"),
        name="ssd",
    )(xbc, zs, dt, a, conv_state, ssm_state, w["cw"], w["cb"], w["dsk"], w["ng"])


def _lambda(lam_ref, lam_init):
    lv = lam_ref[...]
    s01 = jnp.sum(lv[0:1, :] * lv[1:2, :], axis=1, keepdims=True)
    s23 = jnp.sum(lv[2:3, :] * lv[3:4, :], axis=1, keepdims=True)
    return jnp.exp(s01) - jnp.exp(s23) + lam_init


def _softmax_step_t(st, vt, acc, m_sc, l_sc):
    m_prev = m_sc[...]
    m_new = jnp.maximum(m_prev, jnp.max(st, axis=0, keepdims=True))
    alpha = jnp.exp(m_prev - m_new)
    pr = jnp.exp(st - m_new)
    l_sc[...] = alpha * l_sc[...] + jnp.sum(pr, axis=0, keepdims=True)
    acc[...] = alpha * acc[...] + _dot(vt, pr.astype(BF16))
    m_sc[...] = m_new


def _flash_kernel(q_ref, k_ref, vt_ref, mk_ref, mvt_ref, lam_ref, g_ref, o_ref, acc, m_sc, l_sc,
                  *, tq, n_prefix, lam_init):
    i = pl.program_id(2)
    hd = q_ref.shape[1] // 2
    qv = q_ref[...]
    lane = lax.broadcasted_iota(jnp.int32, qv.shape, 1)
    zero = jnp.zeros_like(qv)
    q2 = jnp.concatenate([jnp.where(lane < hd, qv, zero), jnp.where(lane >= hd, qv, zero)], axis=0)

    m_sc[...] = jnp.full(m_sc.shape, -jnp.inf, F32)
    l_sc[...] = jnp.zeros(l_sc.shape, F32)
    acc[...] = jnp.zeros(acc.shape, F32)

    st = _dot_nt(mk_ref[...], q2)
    key = lax.broadcasted_iota(jnp.int32, st.shape, 0)
    _softmax_step_t(jnp.where(key < n_prefix, st, -jnp.inf), mvt_ref[...], acc, m_sc, l_sc)

    def body(c, carry):
        r0 = pl.multiple_of(c * tq, tq)
        _softmax_step_t(_dot_nt(k_ref[pl.ds(r0, tq), :], q2), vt_ref[c], acc, m_sc, l_sc)
        return carry

    lax.fori_loop(0, i, body, 0)

    r0 = pl.multiple_of(i * tq, tq)
    st = _dot_nt(k_ref[pl.ds(r0, tq), :], q2)
    key = lax.broadcasted_iota(jnp.int32, st.shape, 0)
    qry = lax.broadcasted_iota(jnp.int32, st.shape, 1)
    qry = jnp.where(qry >= tq, qry - tq, qry)
    _softmax_step_t(jnp.where(key <= qry, st, -jnp.inf), vt_ref[i], acc, m_sc, l_sc)

    lam = _lambda(lam_ref, lam_init)
    a = acc[...] / l_sc[...]
    o = a[:, 0:tq] - lam * a[:, tq:2 * tq]
    o = o * lax.rsqrt(jnp.mean(o * o, axis=0, keepdims=True) + EPS) * g_ref[...] * (1.0 - lam_init)
    o_ref[...] = o.T.astype(o_ref.dtype)


def _flash_call(qb, kb, vb, mk, mv, w, n, l, lam_init):
    hw = 2 * HEAD_DIM
    nh = ATTN_HEADS
    tq = FLASH_TQ if l % FLASH_TQ == 0 else l
    nq = l // tq
    pk = mk.shape[0]
    vt = vb.reshape(n, nq, tq, nh, hw).transpose(0, 3, 1, 4, 2)
    mvt = mv.reshape(pk, nh, hw).transpose(1, 2, 0)
    return pl.pallas_call(
        functools.partial(_flash_kernel, tq=tq, n_prefix=N_META, lam_init=lam_init),
        grid=(n, nh, nq),
        in_specs=[pl.BlockSpec((tq, hw), lambda b, h, i: (b * nq + i, h)),
                  pl.BlockSpec((l, hw), lambda b, h, i: (b, h)),
                  pl.BlockSpec((None, None, nq, hw, tq), lambda b, h, i: (b, h, 0, 0, 0)),
                  pl.BlockSpec((pk, hw), lambda b, h, i: (0, h)),
                  pl.BlockSpec((None, hw, pk), lambda b, h, i: (h, 0, 0)),
                  _resident((4, HEAD_DIM)), _resident((hw, 1))],
        out_specs=pl.BlockSpec((tq, hw), lambda b, h, i: (b * nq + i, h)),
        out_shape=jax.ShapeDtypeStruct((n * l, nh * hw), BF16),
        scratch_shapes=[pltpu.VMEM((hw, 2 * tq), F32), pltpu.VMEM((1, 2 * tq), F32),
                        pltpu.VMEM((1, 2 * tq), F32)],
        compiler_params=_cparams("parallel", "parallel", "arbitrary"),
        name="flash_attn",
    )(qb, kb, vt, mk, mvt, w["lam"], w["subln"].reshape(hw, 1))


def _short_attn_kernel(*refs, l, nh, n_pages, g_pages, page, lam_init):
    if n_pages:
        pt_ref, q_ref, kn_ref, vn_ref = refs[:4]
        kp_refs = refs[4:4 + g_pages]
        vp_refs = refs[4 + g_pages:4 + 2 * g_pages]
        rest = refs[4 + 2 * g_pages:]
    else:
        q_ref, kn_ref, vn_ref = refs[:3]
        kp_refs = vp_refs = ()
        rest = refs[3:]
    lam_ref, g_ref, o_ref, qm, acc, m_sc, l_sc, kpad, vpad = rest
    step = pl.program_id(1)
    n_steps = n_pages // g_pages if n_pages else 0
    hd = HEAD_DIM
    hw = 2 * hd
    rows = 2 * nh * l
    prow = page * nh

    @pl.when(step == 0)
    def _():
        qv = q_ref[0]
        lane = lax.broadcasted_iota(jnp.int32, (l, hw), 1)
        for h in range(nh):
            qh = qv[:, h * hw:(h + 1) * hw]
            qm[(2 * h) * l:(2 * h + 1) * l, :] = jnp.where(lane < hd, qh, 0.0)
            qm[(2 * h + 1) * l:(2 * h + 2) * l, :] = jnp.where(lane >= hd, qh, 0.0)
        m_sc[...] = jnp.full(m_sc.shape, -jnp.inf, F32)
        l_sc[...] = jnp.zeros(l_sc.shape, F32)
        acc[...] = jnp.zeros(acc.shape, F32)

    row_i = lax.broadcasted_iota(jnp.int32, (rows, prow), 0)
    key_i = lax.broadcasted_iota(jnp.int32, (rows, prow), 1)
    same_head = _div_pow2(row_i, 2 * l) == (key_i & (nh - 1))

    def update(s, vb, keep):
        s = jnp.where(keep, s, -jnp.inf)
        m_prev = m_sc[...]
        m_new = jnp.maximum(m_prev, jnp.max(s, axis=1, keepdims=True))
        alpha = jnp.exp(m_prev - m_new)
        pr = jnp.exp(s - m_new)
        l_sc[...] = alpha * l_sc[...] + jnp.sum(pr, axis=1, keepdims=True)
        acc[...] = alpha * acc[...] + _dot(pr.astype(BF16), vb)
        m_sc[...] = m_new

    if n_pages:
        @pl.when(step < n_steps)
        def _():
            qb = qm[...].astype(BF16)
            for kp_ref, vp_ref in zip(kp_refs, vp_refs):
                update(_dot_nt(qb, kp_ref[...].astype(BF16)), vp_ref[...].astype(BF16), same_head)

    @pl.when(step == n_steps)
    def _():
        kpad[...] = jnp.zeros(kpad.shape, BF16)
        vpad[...] = jnp.zeros(vpad.shape, BF16)
        kpad[0:l * nh, :] = kn_ref[0].astype(BF16)
        vpad[0:l * nh, :] = vn_ref[0].astype(BF16)
        s = _dot_nt(qm[...].astype(BF16), kpad[...])
        t = row_i & (l - 1)
        update(s, vpad[...], same_head & (_div_pow2(key_i, nh) <= t))
        lam = _lambda(lam_ref, lam_init)
        a = acc[...] / l_sc[...]
        for h in range(nh):
            o = a[(2 * h) * l:(2 * h + 1) * l, :] - lam * a[(2 * h + 1) * l:(2 * h + 2) * l, :]
            o_ref[0, :, h * hw:(h + 1) * hw] = _rms(o, g_ref[...]) * (1.0 - lam_init)


def _short_attn_call(q, kn, vn, w, lam_init, cache_k=None, cache_v=None, page_table=None, layer=0):
    n, l, aw = q.shape
    nh = ATTN_HEADS
    hw = 2 * HEAD_DIM
    rows = 2 * nh * l
    page = PAGE_SIZE
    prow = page * nh
    assert l % SUBLANES == 0 and l <= page and l & (l - 1) == 0 and nh & (nh - 1) == 0
    n_pages = 0 if cache_k is None else page_table.shape[1]
    gp = min(PAGES_PER_STEP, n_pages) if n_pages else 0
    assert n_pages == 0 or n_pages % gp == 0
    n_steps = n_pages // gp if n_pages else 0
    seq = lambda b, s, *_: (b, 0, 0)
    in_specs = [pl.BlockSpec((1, l, aw), seq), pl.BlockSpec((1, l * nh, hw), seq),
                pl.BlockSpec((1, l * nh, hw), seq)]
    operands = [q, kn, vn]
    if n_pages:
        def page_map(j):
            def m(b, s, pt):
                return (layer, pt[b, jnp.minimum(s, n_steps - 1) * gp + j], 0, 0)
            return m
        for cache in (cache_k, cache_v):
            for j in range(gp):
                in_specs.append(pl.BlockSpec((None, None, prow, hw), page_map(j)))
                operands.append(cache)
    in_specs += [pl.BlockSpec((4, HEAD_DIM), lambda b, s, *_: (0, 0)),
                 pl.BlockSpec((1, hw), lambda b, s, *_: (0, 0))]
    operands += [w["lam"], w["subln"]]
    grid_spec = pltpu.PrefetchScalarGridSpec(
        num_scalar_prefetch=1 if n_pages else 0,
        grid=(n, n_steps + 1),
        in_specs=in_specs,
        out_specs=pl.BlockSpec((1, l, aw), seq),
        scratch_shapes=[pltpu.VMEM((rows, hw), F32), pltpu.VMEM((rows, hw), F32),
                        pltpu.VMEM((rows, 1), F32), pltpu.VMEM((rows, 1), F32),
                        pltpu.VMEM((prow, hw), BF16), pltpu.VMEM((prow, hw), BF16)],
    )
    call = pl.pallas_call(
        functools.partial(_short_attn_kernel, l=l, nh=nh, n_pages=n_pages, g_pages=gp, page=page,
                          lam_init=lam_init),
        grid_spec=grid_spec,
        out_shape=jax.ShapeDtypeStruct((n, l, aw), F32),
        compiler_params=_cparams("parallel", "arbitrary"),
        name="short_attn",
    )
    return call(page_table, *operands) if n_pages else call(*operands)


def _layer_weights(layer, p):
    d, c, di, cd, aw = D_MODEL, C_CONV, D_INNER, CONV_DIM, ATTN_W
    w_in = p["w_in"][layer]
    o = 0
    cols = {}
    for name, size in (("glu", 2 * c), ("z", di), ("xbc", cd), ("dt", H_SSM), ("q", aw), ("k", aw),
                       ("v", aw), ("g", N_BRANCH * d)):
        cols[name] = w_in[:, o:o + size]
        o += size
    wb = p["w_branch"][layer]
    row = lambda v: v.astype(F32).reshape(1, -1)
    pad_lanes = lambda v: jnp.pad(v, ((0, 0), (0, LANES - v.shape[1])))
    return {
        "wa": cols["glu"][:, :c].astype(BF16), "wb": cols["glu"][:, c:].astype(BF16),
        "wz": cols["z"].astype(BF16), "wxbc": cols["xbc"].astype(BF16),
        "wdt": pad_lanes(cols["dt"]).astype(BF16),
        "dtb": pad_lanes(row(p["ssm_dt_bias"][layer])), "alog": pad_lanes(row(p["ssm_a_log"][layer])),
        "wq": cols["q"].astype(BF16), "wk": cols["k"].astype(BF16), "wv": cols["v"].astype(BF16),
        "wg": cols["g"].astype(BF16), "bg": row(p["b_gate"][layer]),
        "dw_w": p["conv_dw_w"][layer].astype(F32), "dw_b": row(p["conv_dw_b"][layer]),
        "ln_g": row(p["conv_ln_g"][layer]), "ln_b": row(p["conv_ln_b"][layer]),
        "cw": p["ssm_conv_w"][layer].astype(F32), "cb": row(p["ssm_conv_b"][layer]),
        "dsk": row(jnp.repeat(p["ssm_d"][layer], SSM_HEAD_DIM)), "ng": row(p["ssm_norm_g"][layer]),
        "lam": p["attn_lambda"][layer].astype(F32), "subln": row(p["attn_subln_g"][layer]),
        "wbc": wb[:c].astype(BF16), "wbm": wb[c:c + di].astype(BF16), "wba": wb[c + di:].astype(BF16),
        "wo": p["w_o"][layer].astype(BF16), "n2": row(p["norm2_g"][layer]),
        "wu": p["w_up"][layer].astype(BF16), "wd": p["w_down"][layer].astype(BF16),
    }


def _pad_time(x, n, l, lp):
    w = x.shape[1]
    return jnp.pad(x.reshape(n, l, w), ((0, 0), (0, lp - l), (0, 0))).reshape(n * lp, w)


def _short_group(pr, r0, n, l, w, lam_init, conv_state, sconv_state, ssm_state, attn_kwargs):
    r1 = r0 + n * l
    q = SSD_CHUNK
    hp = H_SSM // N_GROUPS
    yc, nconv = _conv_call(pr["glu"][r0:r1], conv_state, w, n, l, F32)
    padded = [_pad_time(pr[name][r0:r1], n, l, q) for name in ("xbc", "zs", "dt", "a")]
    ssm_state = ssm_state.reshape(ssm_state.shape[0], N_GROUPS, hp * SSM_HEAD_DIM, D_STATE)
    ym, nsconv, nssm = _ssd_call(*padded, sconv_state, ssm_state, w, n, q, l, F32)
    ym = ym.reshape(n, q, D_INNER)[:, :l].reshape(n * l, D_INNER)
    q3 = pr["q"][r0:r1].astype(F32).reshape(n, l, ATTN_W)
    per_head = lambda x: x[r0:r1].reshape(n, l * ATTN_HEADS, 2 * HEAD_DIM)
    ya = _short_attn_call(q3, per_head(pr["k32"]), per_head(pr["v32"]), w, lam_init, **attn_kwargs)
    return yc, ym, ya.reshape(n * l, ATTN_W), nconv, nsconv, nssm


def _project(h, w):
    glu, zs, dt, a = _proj1_call(h, w)
    xbc = _proj2_call(h, w)
    q, k32, v32, kb, vb, g = _proj3_call(h, w)
    return {"glu": glu, "zs": zs, "dt": dt, "a": a, "xbc": xbc, "q": q, "k32": k32, "v32": v32,
            "kb": kb, "vb": vb, "g": g}


def kernel(x_prompt, x_sample, cache_k, cache_v, page_table, state_conv, state_ssm_conv, state_ssm, meta_tokens, norm1_g, w_in, b_gate, conv_dw_w, conv_dw_b, conv_ln_g, conv_ln_b, ssm_conv_w, ssm_conv_b, ssm_dt_bias, ssm_a_log, ssm_d, ssm_norm_g, attn_lambda, attn_subln_g, w_branch, w_o, norm2_g, w_up, w_down, final_g):
    params = dict(w_in=w_in, b_gate=b_gate, conv_dw_w=conv_dw_w, conv_dw_b=conv_dw_b, conv_ln_g=conv_ln_g,
                  conv_ln_b=conv_ln_b, ssm_conv_w=ssm_conv_w, ssm_conv_b=ssm_conv_b, ssm_dt_bias=ssm_dt_bias,
                  ssm_a_log=ssm_a_log, ssm_d=ssm_d, ssm_norm_g=ssm_norm_g, attn_lambda=attn_lambda,
                  attn_subln_g=attn_subln_g, w_branch=w_branch, w_o=w_o, norm2_g=norm2_g, w_up=w_up,
                  w_down=w_down)
    nb, seq, d = x_prompt.shape
    ns, ls, _ = x_sample.shape
    nm = meta_tokens.shape[0]
    depth = w_in.shape[0]
    hp = H_SSM // N_GROUPS
    gw = hp * SSM_HEAD_DIM
    assert nm == N_META and nm % SUBLANES == 0

    xm = x_prompt.reshape(nb * seq, d)
    xt = jnp.concatenate([x_sample.reshape(ns * ls, d), meta_tokens.astype(x_sample.dtype)], axis=0)
    rs = ns * ls
    cache_k4 = cache_k.reshape(cache_k.shape[0], cache_k.shape[1], PAGE_SIZE * ATTN_HEADS, 2 * HEAD_DIM)
    cache_v4 = cache_v.reshape(cache_v.shape[0], cache_v.shape[1], PAGE_SIZE * ATTN_HEADS, 2 * HEAD_DIM)

    hm = _rms_call(xm, norm1_g[0])
    ht = _rms_call(xt, norm1_g[0])
    outs_p, outs_s = [], []
    for layer in range(depth):
        w = _layer_weights(layer, params)
        lam_init = 0.8 - 0.6 * math.exp(-0.3 * layer)
        last = layer == depth - 1
        g_next = final_g if last else norm1_g[layer + 1]

        pt = _project(ht, w)
        s_yc, s_ym, s_ya, s_conv, s_sconv, s_ssm = _short_group(
            pt, 0, ns, ls, w, lam_init, state_conv[layer], state_ssm_conv[layer], state_ssm[layer],
            dict(cache_k=cache_k4, cache_v=cache_v4, page_table=page_table, layer=layer))
        m_yc, m_ym, m_ya, m_conv, m_sconv, m_ssm = _short_group(
            pt, rs, 1, nm, w, lam_init,
            jnp.zeros((1, CONV_K - 1, C_CONV), F32), jnp.zeros((1, SSM_CONV_K - 1, CONV_DIM), F32),
            jnp.zeros((1, H_SSM, SSM_HEAD_DIM, D_STATE), F32), {})
        cat = lambda a, b: jnp.concatenate([a, b], axis=0)
        xt1, ht2 = _merge_call(cat(s_yc, m_yc), cat(s_ym, m_ym), cat(s_ya, m_ya), pt["g"], xt, w)
        tail_out = _mlp_call(ht2, xt1, w, g_next, last)

        pm = _project(hm, w)
        p_yc, p_conv = _conv_call(pm["glu"], m_conv, w, nb, seq, BF16)
        p_ym, p_sconv, p_ssm = _ssd_call(pm["xbc"], pm["zs"], pm["dt"], pm["a"], m_sconv, m_ssm, w,
                                         nb, seq, seq, BF16)
        pad_keys = lambda x: jnp.pad(x[rs:rs + nm], ((0, LANES - nm), (0, 0)))
        p_ya = _flash_call(pm["q"], pm["kb"], pm["vb"], pad_keys(pt["kb"]), pad_keys(pt["vb"]), w,
                           nb, seq, lam_init)
        xm1, hm2 = _merge_call(p_yc, p_ym, p_ya, pm["g"], xm, w)
        main_out = _mlp_call(hm2, xm1, w, g_next, last)

        def with_prefix(x_tail, x_main):
            pre = jnp.broadcast_to(x_tail[rs:rs + nm][None], (nb, nm, ATTN_W))
            full = jnp.concatenate([pre, x_main.reshape(nb, seq, ATTN_W)], axis=1)
            return full.reshape(nb, seq + nm, ATTN_HEADS, 2 * HEAD_DIM)

        heads = lambda x: x[:rs].reshape(ns, ls, ATTN_HEADS, 2 * HEAD_DIM)
        state4 = lambda s: s.reshape(s.shape[0], H_SSM, SSM_HEAD_DIM, D_STATE)
        outs_p.append((p_conv, p_sconv, state4(p_ssm), with_prefix(pt["k32"], pm["k32"]),
                       with_prefix(pt["v32"], pm["v32"])))
        outs_s.append((s_conv, s_sconv, state4(s_ssm), heads(pt["k32"]), heads(pt["v32"])))
        if last:
            y_main, y_tail = main_out[0], tail_out[0]
        else:
            (xm, hm), (xt, ht) = main_out, tail_out

    stk = lambda outs, i: jnp.stack([o[i] for o in outs], axis=0)
    y_prompt = y_main.reshape(nb, seq, d)
    y_sample = y_tail[:rs].reshape(ns, ls, d)
    return (y_prompt, y_sample, stk(outs_p, 0), stk(outs_p, 1), stk(outs_p, 2), stk(outs_p, 3), stk(outs_p, 4),
            stk(outs_s, 0), stk(outs_s, 1), stk(outs_s, 2), stk(outs_s, 3), stk(outs_s, 4))
```

```python
import functools
import math

import jax
import jax.numpy as jnp
from jax import lax
from jax.experimental import pallas as pl
from jax.experimental.pallas import tpu as pltpu

F32 = jnp.float32
BF16 = jnp.bfloat16

D_MODEL = 1024
DEPTH = 2
PAGE_SIZE = 128
N_META = 16
C_CONV = D_MODEL
CONV_K = 31
D_INNER = 2 * D_MODEL
SSM_HEAD_DIM = 64
H_SSM = D_INNER // SSM_HEAD_DIM
N_GROUPS = 8
D_STATE = 128
SSM_CONV_K = 4
CONV_DIM = D_INNER + 2 * N_GROUPS * D_STATE
SSD_CHUNK = 128
HEAD_DIM = 64
ATTN_HEADS = D_MODEL // (2 * HEAD_DIM)
ATTN_W = ATTN_HEADS * 2 * HEAD_DIM
D_FF = 4 * D_MODEL
N_BRANCH = 3
EPS = 1e-6

LANES = 128
SUBLANES = 8
VMEM_LIMIT_BYTES = 56 * 1024 * 1024

ROW_TILE = 512
CONV_TL = 128
CONV_HIST = 32
FLASH_TQ = 512
PAGES_PER_STEP = 8
FF_CHUNK = 1024


def _cparams(*sem):
    return pltpu.CompilerParams(dimension_semantics=sem, vmem_limit_bytes=VMEM_LIMIT_BYTES)


def _resident(shape):
    nd = len(shape)
    return pl.BlockSpec(shape, lambda *_: (0,) * nd, pipeline_mode=pl.Buffered(1))


def _rms(x, g):
    return x * lax.rsqrt(jnp.mean(x * x, axis=-1, keepdims=True) + EPS) * g


def _dot(a, b):
    return jnp.dot(a, b, preferred_element_type=F32)


def _dot_nt(a, b):
    return lax.dot_general(a, b, (((1,), (1,)), ((), ())), preferred_element_type=F32)


def _div_pow2(x, n):
    assert n & (n - 1) == 0
    return lax.shift_right_logical(x, n.bit_length() - 1)


def _row_tile(rows):
    return ROW_TILE if rows % ROW_TILE == 0 else rows


def _rms_kernel(x_ref, g_ref, h_ref):
    h_ref[...] = _rms(x_ref[...], g_ref[...]).astype(h_ref.dtype)


def _rms_call(x, g):
    rows, d = x.shape
    tm = _row_tile(rows)
    return pl.pallas_call(
        _rms_kernel,
        grid=(rows // tm,),
        in_specs=[pl.BlockSpec((tm, d), lambda i: (i, 0)), _resident((1, d))],
        out_specs=pl.BlockSpec((tm, d), lambda i: (i, 0)),
        out_shape=jax.ShapeDtypeStruct((rows, d), BF16),
        compiler_params=_cparams("parallel"),
        name="rms0",
    )(x, g.reshape(1, d))


def _proj1_kernel(h_ref, wa_ref, wb_ref, wz_ref, wdt_ref, dtb_ref, alog_ref,
                  glu_ref, zs_ref, dt_ref, a_ref, *, n_heads):
    h = h_ref[...]
    a = _dot(h, wa_ref[...])
    b = _dot(h, wb_ref[...])
    glu_ref[...] = a * jax.nn.sigmoid(b)
    z = _dot(h, wz_ref[...])
    zs_ref[...] = (z * jax.nn.sigmoid(z)).astype(zs_ref.dtype)
    raw = _dot(h, wdt_ref[...]) + dtb_ref[...]
    dt = jnp.maximum(raw, 0.0) + jnp.log1p(jnp.exp(-jnp.abs(raw)))
    lane = lax.broadcasted_iota(jnp.int32, dt.shape, 1)
    dt = jnp.where(lane < n_heads, dt, 0.0)
    dt_ref[...] = dt
    a_ref[...] = dt * (-jnp.exp(alog_ref[...]))


def _proj1_call(h, w):
    rows, d = h.shape
    tm = _row_tile(rows)
    c, di = C_CONV, D_INNER
    row = lambda n: pl.BlockSpec((tm, n), lambda i: (i, 0))
    return pl.pallas_call(
        functools.partial(_proj1_kernel, n_heads=H_SSM),
        grid=(rows // tm,),
        in_specs=[row(d), _resident((d, c)), _resident((d, c)), _resident((d, di)),
                  _resident((d, LANES)), _resident((1, LANES)), _resident((1, LANES))],
        out_specs=[row(c), row(di), row(LANES), row(LANES)],
        out_shape=[jax.ShapeDtypeStruct((rows, c), F32), jax.ShapeDtypeStruct((rows, di), BF16),
                   jax.ShapeDtypeStruct((rows, LANES), F32), jax.ShapeDtypeStruct((rows, LANES), F32)],
        compiler_params=_cparams("parallel"),
        name="proj1",
    )(h, w["wa"], w["wb"], w["wz"], w["wdt"], w["dtb"], w["alog"])


def _proj2_kernel(h_ref, w_ref, o_ref, *, col_chunk):
    h = h_ref[...]
    for c0 in range(0, o_ref.shape[1], col_chunk):
        o_ref[:, c0:c0 + col_chunk] = _dot(h, w_ref[:, c0:c0 + col_chunk]).astype(o_ref.dtype)


def _proj2_call(h, w):
    rows, d = h.shape
    tm = _row_tile(rows)
    cd = CONV_DIM
    return pl.pallas_call(
        functools.partial(_proj2_kernel, col_chunk=min(1024, cd)),
        grid=(rows // tm,),
        in_specs=[pl.BlockSpec((tm, d), lambda i: (i, 0)), _resident((d, cd))],
        out_specs=pl.BlockSpec((tm, cd), lambda i: (i, 0)),
        out_shape=jax.ShapeDtypeStruct((rows, cd), BF16),
        compiler_params=_cparams("parallel"),
        name="proj2",
    )(h, w["wxbc"])


def _proj3_kernel(h_ref, wq_ref, wk_ref, wv_ref, wg_ref, bg_ref,
                  q_ref, k32_ref, v32_ref, kb_ref, vb_ref, g_ref, *, scale, col_chunk):
    h = h_ref[...]
    q_ref[...] = (_dot(h, wq_ref[...]) * scale).astype(q_ref.dtype)
    k = _dot(h, wk_ref[...])
    k32_ref[...] = k
    kb_ref[...] = k.astype(kb_ref.dtype)
    v = _dot(h, wv_ref[...])
    v32_ref[...] = v
    vb_ref[...] = v.astype(vb_ref.dtype)
    for c0 in range(0, g_ref.shape[1], col_chunk):
        g = _dot(h, wg_ref[:, c0:c0 + col_chunk]) + bg_ref[:, c0:c0 + col_chunk]
        g_ref[:, c0:c0 + col_chunk] = jax.nn.sigmoid(g).astype(g_ref.dtype)


def _proj3_call(h, w):
    rows, d = h.shape
    tm = _row_tile(rows)
    aw, gw = ATTN_W, N_BRANCH * D_MODEL
    row = lambda n: pl.BlockSpec((tm, n), lambda i: (i, 0))
    sd = lambda n, dt: jax.ShapeDtypeStruct((rows, n), dt)
    return pl.pallas_call(
        functools.partial(_proj3_kernel, scale=HEAD_DIM ** -0.5, col_chunk=D_MODEL),
        grid=(rows // tm,),
        in_specs=[row(d), _resident((d, aw)), _resident((d, aw)), _resident((d, aw)),
                  _resident((d, gw)), _resident((1, gw))],
        out_specs=[row(aw), row(aw), row(aw), row(aw), row(aw), row(gw)],
        out_shape=[sd(aw, BF16), sd(aw, F32), sd(aw, F32), sd(aw, BF16), sd(aw, BF16), sd(gw, BF16)],
        compiler_params=_cparams("parallel"),
        name="proj3",
    )(h, w["wq"], w["wk"], w["wv"], w["wg"], w["bg"])


def _merge_kernel(yc_ref, ym_ref, ya_ref, g_ref, x_ref, wbc_ref, wbm_ref, wba_ref, wo_ref, n2_ref,
                  x1_ref, h2_ref):
    d = x_ref.shape[1]
    pc = _dot(yc_ref[...].astype(BF16), wbc_ref[...])
    pm = _dot(ym_ref[...].astype(BF16), wbm_ref[...])
    pa = _dot(ya_ref[...].astype(BF16), wba_ref[...])
    merged = (g_ref[:, 0:d].astype(F32) * pc + g_ref[:, d:2 * d].astype(F32) * pm
              + g_ref[:, 2 * d:3 * d].astype(F32) * pa)
    x1 = x_ref[...] + _dot(merged.astype(BF16), wo_ref[...])
    x1_ref[...] = x1
    h2_ref[...] = _rms(x1, n2_ref[...]).astype(h2_ref.dtype)


def _merge_call(yc, ym, ya, g, x, w):
    rows, d = x.shape
    tm = _row_tile(rows)
    c, di, aw = C_CONV, D_INNER, ATTN_W
    row = lambda n: pl.BlockSpec((tm, n), lambda i: (i, 0))
    return pl.pallas_call(
        _merge_kernel,
        grid=(rows // tm,),
        in_specs=[row(c), row(di), row(aw), row(N_BRANCH * d), row(d),
                  _resident((c, d)), _resident((di, d)), _resident((aw, d)), _resident((d, d)),
                  _resident((1, d))],
        out_specs=[row(d), row(d)],
        out_shape=[jax.ShapeDtypeStruct((rows, d), F32), jax.ShapeDtypeStruct((rows, d), BF16)],
        compiler_params=_cparams("parallel"),
        name="merge",
    )(yc, ym, ya, g, x, w["wbc"], w["wbm"], w["wba"], w["wo"], w["n2"])


def _mlp_kernel(h2_ref, x1_ref, wu_ref, wd_ref, gn_ref, *out_refs, ff_chunk, emit_x):
    h2 = h2_ref[...]
    acc = x1_ref[...]
    for c0 in range(0, wu_ref.shape[1], ff_chunk):
        u = _dot(h2, wu_ref[:, c0:c0 + ff_chunk])
        u = jnp.square(jnp.maximum(u, 0.0)).astype(BF16)
        acc = acc + _dot(u, wd_ref[c0:c0 + ff_chunk, :])
    if emit_x:
        out_refs[0][...] = acc
    out_refs[-1][...] = _rms(acc, gn_ref[...]).astype(out_refs[-1].dtype)


def _mlp_call(h2, x1, w, g_next, last):
    rows, d = x1.shape
    tm = _row_tile(rows)
    ff = D_FF
    row = lambda n: pl.BlockSpec((tm, n), lambda i: (i, 0))
    if last:
        out_specs = [row(d)]
        out_shape = [jax.ShapeDtypeStruct((rows, d), F32)]
    else:
        out_specs = [row(d), row(d)]
        out_shape = [jax.ShapeDtypeStruct((rows, d), F32), jax.ShapeDtypeStruct((rows, d), BF16)]
    return pl.pallas_call(
        functools.partial(_mlp_kernel, ff_chunk=min(FF_CHUNK, ff), emit_x=not last),
        grid=(rows // tm,),
        in_specs=[row(d), row(d), _resident((d, ff)), _resident((ff, d)), _resident((1, d))],
        out_specs=out_specs,
        out_shape=out_shape,
        compiler_params=_cparams("parallel"),
        name="mlp",
    )(h2, x1, w["wu"], w["wd"], g_next.reshape(1, d))


def _conv_kernel(x_ref, st_ref, w_ref, b_ref, g_ref, beta_ref, y_ref, nb_ref, win, cbuf, sh,
                 *, tl, kw, rs, cb):
    t = pl.program_id(1)
    nt = pl.num_programs(1)
    c = x_ref.shape[1]
    hist = CONV_HIST
    off = hist - (kw - 1)

    @pl.when(t == 0)
    def _():
        win[0:off, :] = jnp.zeros((off, c), F32)
        win[off:hist, :] = st_ref[0]

    @pl.when(t > 0)
    def _():
        win[0:hist, :] = win[tl:tl + hist, :]

    win[hist:hist + tl, :] = x_ref[...].astype(F32)

    n_sh = hist + tl - SUBLANES
    for s in range(1, SUBLANES):
        sh[s - 1, 0:n_sh, :] = win[s:s + n_sh, :]

    for r0 in range(0, tl, rs):
        for c0 in range(0, c, cb):
            acc = jnp.broadcast_to(b_ref[:, c0:c0 + cb], (rs, cb))
            for k in range(kw):
                s = (off + k) % SUBLANES
                a = r0 + off + k - s
                tap = win[a:a + rs, c0:c0 + cb] if s == 0 else sh[s - 1, a:a + rs, c0:c0 + cb]
                acc = acc + w_ref[k:k + 1, c0:c0 + cb] * tap
            cbuf[r0:r0 + rs, c0:c0 + cb] = acc

    for r0 in range(0, tl, rs):
        v = cbuf[r0:r0 + rs, :]
        mu = jnp.mean(v, axis=-1, keepdims=True)
        xc = v - mu
        y = xc * lax.rsqrt(jnp.mean(xc * xc, axis=-1, keepdims=True) + EPS) * g_ref[...] + beta_ref[...]
        y_ref[r0:r0 + rs, :] = (y * jax.nn.sigmoid(y)).astype(y_ref.dtype)

    @pl.when(t == nt - 1)
    def _():
        nb_ref[0] = win[tl + off:tl + hist, :]


def _conv_call(glu, state, w, n, l, out_dtype):
    c = glu.shape[1]
    kw = CONV_K
    tl = CONV_TL if l % CONV_TL == 0 else l
    nt = l // tl
    rs = min(16, tl)
    cb = min(512, c)
    st_map = (lambda i, t: (i, 0, 0)) if state.shape[0] == n else (lambda i, t: (0, 0, 0))
    return pl.pallas_call(
        functools.partial(_conv_kernel, tl=tl, kw=kw, rs=rs, cb=cb),
        grid=(n, nt),
        in_specs=[pl.BlockSpec((tl, c), lambda i, t: (i * nt + t, 0)),
                  pl.BlockSpec((1, kw - 1, c), st_map),
                  _resident((kw, c)), _resident((1, c)), _resident((1, c)), _resident((1, c))],
        out_specs=[pl.BlockSpec((tl, c), lambda i, t: (i * nt + t, 0)),
                   pl.BlockSpec((1, kw - 1, c), lambda i, t: (i, 0, 0))],
        out_shape=[jax.ShapeDtypeStruct((n * l, c), out_dtype),
                   jax.ShapeDtypeStruct((n, kw - 1, c), F32)],
        scratch_shapes=[pltpu.VMEM((CONV_HIST + tl, c), F32), pltpu.VMEM((tl, c), F32),
                        pltpu.VMEM((SUBLANES - 1, CONV_HIST + tl, c), F32)],
        compiler_params=_cparams("parallel", "arbitrary"),
        name="conv_branch",
    )(glu, state, w["dw_w"], w["dw_b"], w["ln_g"], w["ln_b"])


def _cumsum_rows(x):
    n = x.shape[0]
    row = lax.broadcasted_iota(jnp.int32, x.shape, 0)
    s = 1
    while s < n:
        x = x + jnp.where(row >= s, pltpu.roll(x, s, axis=0), 0.0)
        s *= 2
    return x


def _ssd_kernel(xbc_ref, zs_ref, dt_ref, a_ref, cst_ref, sst_ref, cw_ref, cb_ref, dsk_ref, ng_ref,
                ym_ref, ncs_ref, fst_ref, xw, xc, st,
                *, q, di, ng, ds, hp, p, kc, last_chunk, last_row, rs, ccb):
    c = pl.program_id(1)
    nc = pl.num_programs(1)
    cd = xbc_ref.shape[1]
    gw = hp * p
    pad = SUBLANES

    @pl.when(c == 0)
    def _():
        xw[pad - (kc - 1):pad, :] = cst_ref[0]
        for g in range(ng):
            st[g] = sst_ref[0, g].T

    @pl.when(c > 0)
    def _():
        xw[0:pad, :] = xw[q:q + pad, :]

    xw[pad:pad + q, :] = xbc_ref[...].astype(F32)

    @pl.when(c == last_chunk)
    def _():
        ncs_ref[0] = xw[pad + last_row - (kc - 2):pad + last_row + 1, :]

    for r0 in range(0, q, rs):
        for c0 in range(0, cd, ccb):
            acc = jnp.broadcast_to(cb_ref[:, c0:c0 + ccb], (rs, ccb))
            for k in range(kc):
                o = pad - (kc - 1) + k
                acc = acc + cw_ref[k:k + 1, c0:c0 + ccb] * xw[r0 + o:r0 + o + rs, c0:c0 + ccb]
            xc[r0:r0 + rs, c0:c0 + ccb] = acc * jax.nn.sigmoid(acc)

    dtv = dt_ref[...]
    acs = _cumsum_rows(a_ref[...])
    acs_t = acs.T
    dt_t = dtv.T
    e = jnp.exp(acs)
    dec = jnp.exp(acs[q - 1:q, :])
    wd_t = dt_t * jnp.exp(acs_t[:, q - 1:q] - acs_t)
    ri = lax.broadcasted_iota(jnp.int32, (q, q), 0)
    ci = lax.broadcasted_iota(jnp.int32, (q, q), 1)
    tril = ci <= ri
    lane_head = _div_pow2(lax.broadcasted_iota(jnp.int32, (1, gw), 1), p)

    for g in range(ng):
        b_f = xc[:, di + g * ds:di + (g + 1) * ds]
        c_b = xc[:, di + ng * ds + g * ds:di + ng * ds + (g + 1) * ds].astype(BF16)
        cbm = _dot_nt(c_b, b_f.astype(BF16))
        b_t = b_f.T
        xg = xc[:, g * gw:(g + 1) * gw]
        xgb = xg.astype(BF16)
        yd = jnp.zeros((q, gw), F32)
        sn = jnp.zeros((ds, gw), F32)
        eg = jnp.zeros((q, gw), F32)
        dg = jnp.zeros((1, gw), F32)
        for j in range(hp):
            h = g * hp + j
            mj = lane_head == j
            seg = acs[:, h:h + 1] - acs_t[h:h + 1, :]
            lm = jnp.exp(jnp.where(tril, seg, -jnp.inf))
            wmat = (cbm * lm * dt_t[h:h + 1, :]).astype(BF16)
            xm = jnp.where(mj, xgb, jnp.zeros_like(xgb))
            yd = yd + _dot(wmat, xm)
            ws = (b_t * wd_t[h:h + 1, :]).astype(BF16)
            sn = sn + _dot(ws, xm)
            eg = jnp.where(mj, e[:, h:h + 1], eg)
            dg = jnp.where(mj, dec[:, h:h + 1], dg)
        stg = st[g]
        yo = _dot(c_b, stg.astype(BF16)) * eg
        st[g] = stg * dg + sn
        y = yd + yo + xg * dsk_ref[:, g * gw:(g + 1) * gw]
        y = y * zs_ref[:, g * gw:(g + 1) * gw].astype(F32)
        y = _rms(y, ng_ref[:, g * gw:(g + 1) * gw])
        ym_ref[:, g * gw:(g + 1) * gw] = y.astype(ym_ref.dtype)

    @pl.when(c == nc - 1)
    def _():
        for g in range(ng):
            fst_ref[0, g] = st[g].T


def _ssd_call(xbc, zs, dt, a, conv_state, ssm_state, w, n, l, l_valid, out_dtype):
    q = SSD_CHUNK
    nc = l // q
    cd, di, ng, ds, p = CONV_DIM, D_INNER, N_GROUPS, D_STATE, SSM_HEAD_DIM
    hp = H_SSM // ng
    gw = hp * p
    kc = SSM_CONV_K
    last_chunk, last_row = (l_valid - 1) // q, (l_valid - 1) % q
    assert last_row >= kc - 2 and H_SSM <= LANES
    cs_map = (lambda i, c: (i, 0, 0)) if conv_state.shape[0] == n else (lambda i, c: (0, 0, 0))
    ss_map = (lambda i, c: (i, 0, 0, 0)) if ssm_state.shape[0] == n else (lambda i, c: (0, 0, 0, 0))
    row = lambda w_: pl.BlockSpec((q, w_), lambda i, c: (i * nc + c, 0))
    return pl.pallas_call(
        functools.partial(_ssd_kernel, q=q, di=di, ng=ng, ds=ds, hp=hp, p=p, kc=kc,
                          last_chunk=last_chunk, last_row=last_row, rs=32, ccb=min(512, cd)),
        grid=(n, nc),
        in_specs=[row(cd), row(di), row(LANES), row(LANES),
                  pl.BlockSpec((1, kc - 1, cd), cs_map),
                  pl.BlockSpec((1, ng, gw, ds), ss_map),
                  _resident((kc, cd)), _resident((1, cd)), _resident((1, di)), _resident((1, di))],
        out_specs=[row(di),
                   pl.BlockSpec((1, kc - 1, cd), lambda i, c: (i, 0, 0)),
                   pl.BlockSpec((1, ng, gw, ds), lambda i, c: (i, 0, 0, 0))],
        out_shape=[jax.ShapeDtypeStruct((n * l, di), out_dtype),
                   jax.ShapeDtypeStruct((n, kc - 1, cd), F32),
                   jax.ShapeDtypeStruct((n, ng, gw, ds), F32)],
        scratch_shapes=[pltpu.VMEM((SUBLANES + q, cd), F32), pltpu.VMEM((q, cd), F32),
                        pltpu.VMEM((ng, ds, gw), F32)],
        compiler_params=_cparams("parallel", "arbitrary"),
        name="ssd",
    )(xbc, zs, dt, a, conv_state, ssm_state, w["cw"], w["cb"], w["dsk"], w["ng"])


def _lambda(lam_ref, lam_init):
    lv = lam_ref[...]
    s01 = jnp.sum(lv[0:1, :] * lv[1:2, :], axis=1, keepdims=True)
    s23 = jnp.sum(lv[2:3, :] * lv[3:4, :], axis=1, keepdims=True)
    return jnp.exp(s01) - jnp.exp(s23) + lam_init


def _softmax_step_t(st, vt, acc, m_sc, l_sc):
    m_prev = m_sc[...]
    m_new = jnp.maximum(m_prev, jnp.max(st, axis=0, keepdims=True))
    alpha = jnp.exp(m_prev - m_new)
    pr = jnp.exp(st - m_new)
    l_sc[...] = alpha * l_sc[...] + jnp.sum(pr, axis=0, keepdims=True)
    acc[...] = alpha * acc[...] + _dot(vt, pr.astype(BF16))
    m_sc[...] = m_new


def _flash_kernel(q_ref, k_ref, vt_ref, mk_ref, mvt_ref, lam_ref, g_ref, o_ref, acc, m_sc, l_sc,
                  *, tq, n_prefix, lam_init):
    i = pl.program_id(2)
    hd = q_ref.shape[1] // 2
    qv = q_ref[...]
    lane = lax.broadcasted_iota(jnp.int32, qv.shape, 1)
    zero = jnp.zeros_like(qv)
    q2 = jnp.concatenate([jnp.where(lane < hd, qv, zero), jnp.where(lane >= hd, qv, zero)], axis=0)

    m_sc[...] = jnp.full(m_sc.shape, -jnp.inf, F32)
    l_sc[...] = jnp.zeros(l_sc.shape, F32)
    acc[...] = jnp.zeros(acc.shape, F32)

    st = _dot_nt(mk_ref[...], q2)
    key = lax.broadcasted_iota(jnp.int32, st.shape, 0)
    _softmax_step_t(jnp.where(key < n_prefix, st, -jnp.inf), mvt_ref[...], acc, m_sc, l_sc)

    def body(c, carry):
        r0 = pl.multiple_of(c * tq, tq)
        _softmax_step_t(_dot_nt(k_ref[pl.ds(r0, tq), :], q2), vt_ref[c], acc, m_sc, l_sc)
        return carry

    lax.fori_loop(0, i, body, 0)

    r0 = pl.multiple_of(i * tq, tq)
    st = _dot_nt(k_ref[pl.ds(r0, tq), :], q2)
    key = lax.broadcasted_iota(jnp.int32, st.shape, 0)
    qry = lax.broadcasted_iota(jnp.int32, st.shape, 1)
    qry = jnp.where(qry >= tq, qry - tq, qry)
    _softmax_step_t(jnp.where(key <= qry, st, -jnp.inf), vt_ref[i], acc, m_sc, l_sc)

    lam = _lambda(lam_ref, lam_init)
    a = acc[...] / l_sc[...]
    o = a[:, 0:tq] - lam * a[:, tq:2 * tq]
    o = o * lax.rsqrt(jnp.mean(o * o, axis=0, keepdims=True) + EPS) * g_ref[...] * (1.0 - lam_init)
    o_ref[...] = o.T.astype(o_ref.dtype)


def _flash_call(qb, kb, vb, mk, mv, w, n, l, lam_init):
    hw = 2 * HEAD_DIM
    nh = ATTN_HEADS
    tq = FLASH_TQ if l % FLASH_TQ == 0 else l
    nq = l // tq
    pk = mk.shape[0]
    vt = vb.reshape(n, nq, tq, nh, hw).transpose(0, 3, 1, 4, 2)
    mvt = mv.reshape(pk, nh, hw).transpose(1, 2, 0)
    return pl.pallas_call(
        functools.partial(_flash_kernel, tq=tq, n_prefix=N_META, lam_init=lam_init),
        grid=(n, nh, nq),
        in_specs=[pl.BlockSpec((tq, hw), lambda b, h, i: (b * nq + i, h)),
                  pl.BlockSpec((l, hw), lambda b, h, i: (b, h)),
                  pl.BlockSpec((None, None, nq, hw, tq), lambda b, h, i: (b, h, 0, 0, 0)),
                  pl.BlockSpec((pk, hw), lambda b, h, i: (0, h)),
                  pl.BlockSpec((None, hw, pk), lambda b, h, i: (h, 0, 0)),
                  _resident((4, HEAD_DIM)), _resident((hw, 1))],
        out_specs=pl.BlockSpec((tq, hw), lambda b, h, i: (b * nq + i, h)),
        out_shape=jax.ShapeDtypeStruct((n * l, nh * hw), BF16),
        scratch_shapes=[pltpu.VMEM((hw, 2 * tq), F32), pltpu.VMEM((1, 2 * tq), F32),
                        pltpu.VMEM((1, 2 * tq), F32)],
        compiler_params=_cparams("parallel", "parallel", "arbitrary"),
        name="flash_attn",
    )(qb, kb, vt, mk, mvt, w["lam"], w["subln"].reshape(hw, 1))


def _short_attn_kernel(*refs, l, nh, n_pages, g_pages, page, lam_init):
    if n_pages:
        pt_ref, q_ref, kn_ref, vn_ref = refs[:4]
        kp_refs = refs[4:4 + g_pages]
        vp_refs = refs[4 + g_pages:4 + 2 * g_pages]
        rest = refs[4 + 2 * g_pages:]
    else:
        q_ref, kn_ref, vn_ref = refs[:3]
        kp_refs = vp_refs = ()
        rest = refs[3:]
    lam_ref, g_ref, o_ref, qm, acc, m_sc, l_sc, kpad, vpad = rest
    step = pl.program_id(1)
    n_steps = n_pages // g_pages if n_pages else 0
    hd = HEAD_DIM
    hw = 2 * hd
    rows = 2 * nh * l
    prow = page * nh

    @pl.when(step == 0)
    def _():
        qv = q_ref[0]
        lane = lax.broadcasted_iota(jnp.int32, (l, hw), 1)
        for h in range(nh):
            qh = qv[:, h * hw:(h + 1) * hw]
            qm[(2 * h) * l:(2 * h + 1) * l, :] = jnp.where(lane < hd, qh, 0.0)
            qm[(2 * h + 1) * l:(2 * h + 2) * l, :] = jnp.where(lane >= hd, qh, 0.0)
        m_sc[...] = jnp.full(m_sc.shape, -jnp.inf, F32)
        l_sc[...] = jnp.zeros(l_sc.shape, F32)
        acc[...] = jnp.zeros(acc.shape, F32)

    row_i = lax.broadcasted_iota(jnp.int32, (rows, prow), 0)
    key_i = lax.broadcasted_iota(jnp.int32, (rows, prow), 1)
    same_head = _div_pow2(row_i, 2 * l) == (key_i & (nh - 1))

    def update(scores, values, keep):
        ss = [jnp.where(keep, s, -jnp.inf) for s in scores]
        m_prev = m_sc[...]
        m_new = jnp.maximum(m_prev, jnp.max(functools.reduce(jnp.maximum, ss), axis=1, keepdims=True))
        alpha = jnp.exp(m_prev - m_new)
        ps = [jnp.exp(s - m_new) for s in ss]
        l_sc[...] = alpha * l_sc[...] + jnp.sum(functools.reduce(jnp.add, ps), axis=1, keepdims=True)
        pv = functools.reduce(jnp.add, [_dot(p.astype(BF16), v) for p, v in zip(ps, values)])
        acc[...] = alpha * acc[...] + pv
        m_sc[...] = m_new

    if n_pages:
        @pl.when(step < n_steps)
        def _():
            qb = qm[...].astype(BF16)
            update([_dot_nt(qb, kp_ref[...].astype(BF16)) for kp_ref in kp_refs],
                   [vp_ref[...].astype(BF16) for vp_ref in vp_refs], same_head)

    @pl.when(step == n_steps)
    def _():
        kpad[...] = jnp.zeros(kpad.shape, BF16)
        vpad[...] = jnp.zeros(vpad.shape, BF16)
        kpad[0:l * nh, :] = kn_ref[0].astype(BF16)
        vpad[0:l * nh, :] = vn_ref[0].astype(BF16)
        s = _dot_nt(qm[...].astype(BF16), kpad[...])
        t = row_i & (l - 1)
        update([s], [vpad[...]], same_head & (_div_pow2(key_i, nh) <= t))
        lam = _lambda(lam_ref, lam_init)
        a = acc[...] / l_sc[...]
        for h in range(nh):
            o = a[(2 * h) * l:(2 * h + 1) * l, :] - lam * a[(2 * h + 1) * l:(2 * h + 2) * l, :]
            o_ref[0, :, h * hw:(h + 1) * hw] = _rms(o, g_ref[...]) * (1.0 - lam_init)


def _short_attn_call(q, kn, vn, w, lam_init, cache_k=None, cache_v=None, page_table=None, layer=0):
    n, l, aw = q.shape
    nh = ATTN_HEADS
    hw = 2 * HEAD_DIM
    rows = 2 * nh * l
    page = PAGE_SIZE
    prow = page * nh
    assert l % SUBLANES == 0 and l <= page and l & (l - 1) == 0 and nh & (nh - 1) == 0
    n_pages = 0 if cache_k is None else page_table.shape[1]
    gp = min(PAGES_PER_STEP, n_pages) if n_pages else 0
    assert n_pages == 0 or n_pages % gp == 0
    n_steps = n_pages // gp if n_pages else 0
    seq = lambda b, s, *_: (b, 0, 0)
    in_specs = [pl.BlockSpec((1, l, aw), seq), pl.BlockSpec((1, l * nh, hw), seq),
                pl.BlockSpec((1, l * nh, hw), seq)]
    operands = [q, kn, vn]
    if n_pages:
        def page_map(j):
            def m(b, s, pt):
                return (layer, pt[b, jnp.minimum(s, n_steps - 1) * gp + j], 0, 0)
            return m
        for cache in (cache_k, cache_v):
            for j in range(gp):
                in_specs.append(pl.BlockSpec((None, None, prow, hw), page_map(j)))
                operands.append(cache)
    in_specs += [pl.BlockSpec((4, HEAD_DIM), lambda b, s, *_: (0, 0)),
                 pl.BlockSpec((1, hw), lambda b, s, *_: (0, 0))]
    operands += [w["lam"], w["subln"]]
    grid_spec = pltpu.PrefetchScalarGridSpec(
        num_scalar_prefetch=1 if n_pages else 0,
        grid=(n, n_steps + 1),
        in_specs=in_specs,
        out_specs=pl.BlockSpec((1, l, aw), seq),
        scratch_shapes=[pltpu.VMEM((rows, hw), F32), pltpu.VMEM((rows, hw), F32),
                        pltpu.VMEM((rows, 1), F32), pltpu.VMEM((rows, 1), F32),
                        pltpu.VMEM((prow, hw), BF16), pltpu.VMEM((prow, hw), BF16)],
    )
    call = pl.pallas_call(
        functools.partial(_short_attn_kernel, l=l, nh=nh, n_pages=n_pages, g_pages=gp, page=page,
                          lam_init=lam_init),
        grid_spec=grid_spec,
        out_shape=jax.ShapeDtypeStruct((n, l, aw), F32),
        compiler_params=_cparams("parallel", "arbitrary"),
        name="short_attn",
    )
    return call(page_table, *operands) if n_pages else call(*operands)


def _layer_weights(layer, p):
    d, c, di, cd, aw = D_MODEL, C_CONV, D_INNER, CONV_DIM, ATTN_W
    w_in = p["w_in"][layer]
    o = 0
    cols = {}
    for name, size in (("glu", 2 * c), ("z", di), ("xbc", cd), ("dt", H_SSM), ("q", aw), ("k", aw),
                       ("v", aw), ("g", N_BRANCH * d)):
        cols[name] = w_in[:, o:o + size]
        o += size
    wb = p["w_branch"][layer]
    row = lambda v: v.astype(F32).reshape(1, -1)
    pad_lanes = lambda v: jnp.pad(v, ((0, 0), (0, LANES - v.shape[1])))
    return {
        "wa": cols["glu"][:, :c].astype(BF16), "wb": cols["glu"][:, c:].astype(BF16),
        "wz": cols["z"].astype(BF16), "wxbc": cols["xbc"].astype(BF16),
        "wdt": pad_lanes(cols["dt"]).astype(BF16),
        "dtb": pad_lanes(row(p["ssm_dt_bias"][layer])), "alog": pad_lanes(row(p["ssm_a_log"][layer])),
        "wq": cols["q"].astype(BF16), "wk": cols["k"].astype(BF16), "wv": cols["v"].astype(BF16),
        "wg": cols["g"].astype(BF16), "bg": row(p["b_gate"][layer]),
        "dw_w": p["conv_dw_w"][layer].astype(F32), "dw_b": row(p["conv_dw_b"][layer]),
        "ln_g": row(p["conv_ln_g"][layer]), "ln_b": row(p["conv_ln_b"][layer]),
        "cw": p["ssm_conv_w"][layer].astype(F32), "cb": row(p["ssm_conv_b"][layer]),
        "dsk": row(jnp.repeat(p["ssm_d"][layer], SSM_HEAD_DIM)), "ng": row(p["ssm_norm_g"][layer]),
        "lam": p["attn_lambda"][layer].astype(F32), "subln": row(p["attn_subln_g"][layer]),
        "wbc": wb[:c].astype(BF16), "wbm": wb[c:c + di].astype(BF16), "wba": wb[c + di:].astype(BF16),
        "wo": p["w_o"][layer].astype(BF16), "n2": row(p["norm2_g"][layer]),
        "wu": p["w_up"][layer].astype(BF16), "wd": p["w_down"][layer].astype(BF16),
    }


def _pad_time(x, n, l, lp):
    w = x.shape[1]
    return jnp.pad(x.reshape(n, l, w), ((0, 0), (0, lp - l), (0, 0))).reshape(n * lp, w)


def _short_group(pr, r0, n, l, w, lam_init, conv_state, sconv_state, ssm_state, attn_kwargs):
    r1 = r0 + n * l
    q = SSD_CHUNK
    hp = H_SSM // N_GROUPS
    yc, nconv = _conv_call(pr["glu"][r0:r1], conv_state, w, n, l, F32)
    padded = [_pad_time(pr[name][r0:r1], n, l, q) for name in ("xbc", "zs", "dt", "a")]
    ssm_state = ssm_state.reshape(ssm_state.shape[0], N_GROUPS, hp * SSM_HEAD_DIM, D_STATE)
    ym, nsconv, nssm = _ssd_call(*padded, sconv_state, ssm_state, w, n, q, l, F32)
    ym = ym.reshape(n, q, D_INNER)[:, :l].reshape(n * l, D_INNER)
    q3 = pr["q"][r0:r1].astype(F32).reshape(n, l, ATTN_W)
    per_head = lambda x: x[r0:r1].reshape(n, l * ATTN_HEADS, 2 * HEAD_DIM)
    ya = _short_attn_call(q3, per_head(pr["k32"]), per_head(pr["v32"]), w, lam_init, **attn_kwargs)
    return yc, ym, ya.reshape(n * l, ATTN_W), nconv, nsconv, nssm


def _project(h, w):
    glu, zs, dt, a = _proj1_call(h, w)
    xbc = _proj2_call(h, w)
    q, k32, v32, kb, vb, g = _proj3_call(h, w)
    return {"glu": glu, "zs": zs, "dt": dt, "a": a, "xbc": xbc, "q": q, "k32": k32, "v32": v32,
            "kb": kb, "vb": vb, "g": g}


def kernel(x_prompt, x_sample, cache_k, cache_v, page_table, state_conv, state_ssm_conv, state_ssm, meta_tokens, norm1_g, w_in, b_gate, conv_dw_w, conv_dw_b, conv_ln_g, conv_ln_b, ssm_conv_w, ssm_conv_b, ssm_dt_bias, ssm_a_log, ssm_d, ssm_norm_g, attn_lambda, attn_subln_g, w_branch, w_o, norm2_g, w_up, w_down, final_g):
    params = dict(w_in=w_in, b_gate=b_gate, conv_dw_w=conv_dw_w, conv_dw_b=conv_dw_b, conv_ln_g=conv_ln_g,
                  conv_ln_b=conv_ln_b, ssm_conv_w=ssm_conv_w, ssm_conv_b=ssm_conv_b, ssm_dt_bias=ssm_dt_bias,
                  ssm_a_log=ssm_a_log, ssm_d=ssm_d, ssm_norm_g=ssm_norm_g, attn_lambda=attn_lambda,
                  attn_subln_g=attn_subln_g, w_branch=w_branch, w_o=w_o, norm2_g=norm2_g, w_up=w_up,
                  w_down=w_down)
    nb, seq, d = x_prompt.shape
    ns, ls, _ = x_sample.shape
    nm = meta_tokens.shape[0]
    depth = w_in.shape[0]
    hp = H_SSM // N_GROUPS
    gw = hp * SSM_HEAD_DIM
    assert nm == N_META and nm % SUBLANES == 0

    xm = x_prompt.reshape(nb * seq, d)
    xt = jnp.concatenate([x_sample.reshape(ns * ls, d), meta_tokens.astype(x_sample.dtype)], axis=0)
    rs = ns * ls
    cache_k4 = cache_k.reshape(cache_k.shape[0], cache_k.shape[1], PAGE_SIZE * ATTN_HEADS, 2 * HEAD_DIM)
    cache_v4 = cache_v.reshape(cache_v.shape[0], cache_v.shape[1], PAGE_SIZE * ATTN_HEADS, 2 * HEAD_DIM)

    hm = _rms_call(xm, norm1_g[0])
    ht = _rms_call(xt, norm1_g[0])
    outs_p, outs_s = [], []
    for layer in range(depth):
        w = _layer_weights(layer, params)
        lam_init = 0.8 - 0.6 * math.exp(-0.3 * layer)
        last = layer == depth - 1
        g_next = final_g if last else norm1_g[layer + 1]

        pt = _project(ht, w)
        s_yc, s_ym, s_ya, s_conv, s_sconv, s_ssm = _short_group(
            pt, 0, ns, ls, w, lam_init, state_conv[layer], state_ssm_conv[layer], state_ssm[layer],
            dict(cache_k=cache_k4, cache_v=cache_v4, page_table=page_table, layer=layer))
        m_yc, m_ym, m_ya, m_conv, m_sconv, m_ssm = _short_group(
            pt, rs, 1, nm, w, lam_init,
            jnp.zeros((1, CONV_K - 1, C_CONV), F32), jnp.zeros((1, SSM_CONV_K - 1, CONV_DIM), F32),
            jnp.zeros((1, H_SSM, SSM_HEAD_DIM, D_STATE), F32), {})
        cat = lambda a, b: jnp.concatenate([a, b], axis=0)
        xt1, ht2 = _merge_call(cat(s_yc, m_yc), cat(s_ym, m_ym), cat(s_ya, m_ya), pt["g"], xt, w)
        tail_out = _mlp_call(ht2, xt1, w, g_next, last)

        pm = _project(hm, w)
        p_yc, p_conv = _conv_call(pm["glu"], m_conv, w, nb, seq, BF16)
        p_ym, p_sconv, p_ssm = _ssd_call(pm["xbc"], pm["zs"], pm["dt"], pm["a"], m_sconv, m_ssm, w,
                                         nb, seq, seq, BF16)
        pad_keys = lambda x: jnp.pad(x[rs:rs + nm], ((0, LANES - nm), (0, 0)))
        p_ya = _flash_call(pm["q"], pm["kb"], pm["vb"], pad_keys(pt["kb"]), pad_keys(pt["vb"]), w,
                           nb, seq, lam_init)
        xm1, hm2 = _merge_call(p_yc, p_ym, p_ya, pm["g"], xm, w)
        main_out = _mlp_call(hm2, xm1, w, g_next, last)

        def with_prefix(x_tail, x_main):
            pre = jnp.broadcast_to(x_tail[rs:rs + nm][None], (nb, nm, ATTN_W))
            full = jnp.concatenate([pre, x_main.reshape(nb, seq, ATTN_W)], axis=1)
            return full.reshape(nb, seq + nm, ATTN_HEADS, 2 * HEAD_DIM)

        heads = lambda x: x[:rs].reshape(ns, ls, ATTN_HEADS, 2 * HEAD_DIM)
        state4 = lambda s: s.reshape(s.shape[0], H_SSM, SSM_HEAD_DIM, D_STATE)
        outs_p.append((p_conv, p_sconv, state4(p_ssm), with_prefix(pt["k32"], pm["k32"]),
                       with_prefix(pt["v32"], pm["v32"])))
        outs_s.append((s_conv, s_sconv, state4(s_ssm), heads(pt["k32"]), heads(pt["v32"])))
        if last:
            y_main, y_tail = main_out[0], tail_out[0]
        else:
            (xm, hm), (xt, ht) = main_out, tail_out

    stk = lambda outs, i: jnp.stack([o[i] for o in outs], axis=0)
    y_prompt = y_main.reshape(nb, seq, d)
    y_sample = y_tail[:rs].reshape(ns, ls, d)
    return (y_prompt, y_sample, stk(outs_p, 0), stk(outs_p, 1), stk(outs_p, 2), stk(outs_p, 3), stk(outs_p, 4),
            stk(outs_s, 0), stk(outs_s, 1), stk(outs_s, 2), stk(outs_s, 3), stk(outs_s, 4))
```
